```python
import jax
import jax.numpy as jnp
from jax import lax
import numpy as np

D_MODEL = 2048
BATCH = 2
SEQ = 16384
DEPTH = 1

CHUNK = 64
MEM_LEN = 256
Q_BLOCK = 128
TOKEN_BLOCK = 128
EPS = 1e-6
ROPE_THETA = 10000.0

A_WIDTH = D_MODEL // 2
A_HEADS = 8
A_HEAD_DIM = A_WIDTH // A_HEADS
IDX_HEADS = 16
IDX_DIM = 64
TOP_K_MAX = 256

POOL_WIDTH = D_MODEL // 4
POOL_WINDOWS = (2, 4, 8, 16)
POOL_GROUP = POOL_WIDTH // 4

M_WIDTH = D_MODEL // 4
M_HEADS = 4
M_HEAD_DIM = M_WIDTH // M_HEADS

IN_SPLITS = (A_WIDTH, A_WIDTH, A_WIDTH, IDX_HEADS * IDX_DIM, IDX_DIM, IDX_HEADS, POOL_WIDTH, M_WIDTH)
IN_WIDTH = 3 * A_WIDTH + IDX_HEADS * IDX_DIM + IDX_DIM + IDX_HEADS + POOL_WIDTH + M_WIDTH
MIX_WIDTH = A_WIDTH + POOL_WIDTH + M_WIDTH

PEER_HEADS = 8
N_KEYS = 128
N_EXPERTS = N_KEYS * N_KEYS
PEER_TOPK = 16
PEER_KEY_DIM = 256

kernel_name = 'hybrid_dsa_pool_mem_peer_block'


def rms_norm(x, g):
    xf = x.astype(jnp.float32)
    y = xf * lax.rsqrt(jnp.mean(xf * xf, axis=-1, keepdims=True) + EPS)
    return (y * g.astype(jnp.float32)).astype(x.dtype)


def apply_rope(x, pos):
    half = x.shape[-1] // 2
    inv_freq = ROPE_THETA ** (-jnp.arange(half, dtype=jnp.float32) / half)
    ang = pos.astype(jnp.float32)[:, None] * inv_freq[None, :]
    cos = jnp.cos(ang)[None, :, None, :]
    sin = jnp.sin(ang)[None, :, None, :]
    xf = x.astype(jnp.float32)
    x1, x2 = xf[..., :half], xf[..., half:]
    return jnp.concatenate([x1 * cos - x2 * sin, x2 * cos + x1 * sin], axis=-1).astype(x.dtype)


def split_columns(z, widths):
    outs = []
    start = 0
    for w in widths:
        outs.append(z[..., start:start + w])
        start += w
    return outs


def dsa_sparse_attention(q, k, v, q_idx, k_idx, w_idx, pos):
    b, s, h, dh = q.shape
    top_k = min(TOP_K_MAX, s // 4)
    n_blk = s // Q_BLOCK
    key_chunk = pos // CHUNK
    scale = dh ** -0.5
    idx_scale = IDX_DIM ** -0.5

    def to_blocks(t):
        return jnp.swapaxes(t.reshape((b, n_blk, Q_BLOCK) + t.shape[2:]), 0, 1)

    def block(args):
        qb, qib, wb, qpos = args
        q_chunk = qpos // CHUNK
        raw = jnp.einsum('bqhd,bsd->bqhs', qib, k_idx) * idx_scale
        score = jnp.einsum('bqhs,bqh->bqs', jax.nn.relu(raw), wb).astype(jnp.float32)
        admissible = key_chunk[None, :] <= q_chunk[:, None]
        score = jnp.where(admissible[None], score, -jnp.inf)
        _, sel = lax.top_k(score, top_k)
        kg = jax.vmap(lambda kk, ii: kk[ii])(k, sel)
        vg = jax.vmap(lambda vv, ii: vv[ii])(v, sel)
        valid = (sel // CHUNK) <= q_chunk[None, :, None]
        logits = jnp.einsum('bqhd,bqkhd->bhqk', qb, kg).astype(jnp.float32) * scale
        logits = jnp.where(valid[:, None], logits, -jnp.inf)
        p = jax.nn.softmax(logits, axis=-1).astype(vg.dtype)
        return jnp.einsum('bhqk,bqkhd->bqhd', p, vg)

    out = lax.map(block, (to_blocks(q), to_blocks(q_idx), to_blocks(w_idx),
                          pos.reshape(n_blk, Q_BLOCK)))
    return jnp.swapaxes(out, 0, 1).reshape(b, s, h * dh)


def multiscale_pool_mixer(u, pool_w, pool_scale):
    b, s, c = u.shape
    uf = u.astype(jnp.float32)
    cs = jnp.concatenate([jnp.zeros((b, 1, c), jnp.float32), jnp.cumsum(uf, axis=1)], axis=1)
    t1 = jnp.arange(1, s + 1)
    outs = []
    for g, w in enumerate(POOL_WINDOWS):
        lo_c, hi_c = g * POOL_GROUP, (g + 1) * POOL_GROUP
        lo = jnp.maximum(t1 - w, 0)
        cnt = (t1 - lo).astype(jnp.float32)[None, :, None]
        csg = cs[:, :, lo_c:hi_c]
        mean = (csg[:, 1:] - csg[:, lo]) / cnt
        d = (mean - uf[:, :, lo_c:hi_c]).astype(u.dtype)
        outs.append(jnp.einsum('bsc,ce->bse', d, pool_w[g]))
    return jnp.concatenate(outs, axis=-1) * pool_scale


def memory_cross_attention(qm, km, vm):
    logits = jnp.einsum('bshd,bmhd->bhsm', qm, km).astype(jnp.float32) * (qm.shape[-1] ** -0.5)
    p = jax.nn.softmax(logits, axis=-1).astype(vm.dtype)
    out = jnp.einsum('bhsm,bmhd->bshd', p, vm)
    return out.reshape(qm.shape[0], qm.shape[1], -1)


def peer_ffn(x, w_q, sub_keys_1, sub_keys_2, u_tab, v_tab):
    b, s, d = x.shape
    n_tok = b * s
    t = x.reshape(n_tok, d)
    q = (t @ w_q).reshape(n_tok, PEER_HEADS, PEER_KEY_DIM)
    half = PEER_KEY_DIM // 2
    s1 = jnp.einsum('thd,nd->thn', q[..., :half], sub_keys_1).astype(jnp.float32)
    s2 = jnp.einsum('thd,nd->thn', q[..., half:], sub_keys_2).astype(jnp.float32)
    v1, i1 = lax.top_k(s1, PEER_TOPK)
    v2, i2 = lax.top_k(s2, PEER_TOPK)
    cand = (v1[..., :, None] + v2[..., None, :]).reshape(n_tok, PEER_HEADS, PEER_TOPK * PEER_TOPK)
    sc, ci = lax.top_k(cand, PEER_TOPK)
    e = (jnp.take_along_axis(i1, ci // PEER_TOPK, axis=-1) * N_KEYS
         + jnp.take_along_axis(i2, ci % PEER_TOPK, axis=-1))
    g = jax.nn.softmax(sc, axis=-1).astype(x.dtype)
    n_blk = n_tok // TOKEN_BLOCK

    def block(args):
        tb, eb, gb = args
        h = jnp.einsum('td,ted->te', tb, u_tab[eb])
        a = gb * jax.nn.gelu(h, approximate=False)
        return jnp.einsum('te,ted->td', a, v_tab[eb])

    out = lax.map(block, (t.reshape(n_blk, TOKEN_BLOCK, d),
                          e.reshape(n_blk, TOKEN_BLOCK, PEER_HEADS * PEER_TOPK),
                          g.reshape(n_blk, TOKEN_BLOCK, PEER_HEADS * PEER_TOPK)))
    return out.reshape(b, s, d)


def setup_inputs(seed: int = 0) -> dict:
    key = jax.random.key(seed)
    ks = jax.random.split(key, 20)
    f32 = jnp.float32

    def nrm(k, shape, scale):
        return jax.random.normal(k, shape, f32) * scale

    def gain(k, shape):
        return 1.0 + 0.02 * jax.random.normal(k, shape, f32)

    L = DEPTH
    return {
        'x': jax.random.normal(ks[0], (BATCH, SEQ, D_MODEL), f32),
        'mem': jax.random.normal(ks[1], (BATCH, MEM_LEN, D_MODEL), f32),
        'attn_norm_gain': gain(ks[2], (L, D_MODEL)),
        'w_in': nrm(ks[3], (L, D_MODEL, IN_WIDTH), D_MODEL ** -0.5),
        'q_norm_gain': gain(ks[4], (L, A_HEAD_DIM)),
        'k_norm_gain': gain(ks[5], (L, A_HEAD_DIM)),
        'pool_w': nrm(ks[6], (L, len(POOL_WINDOWS), POOL_GROUP, POOL_GROUP), POOL_GROUP ** -0.5),
        'pool_scale': gain(ks[7], (L, POOL_WIDTH)),
        'mem_norm_gain': gain(ks[8], (L, D_MODEL)),
        'w_mem_kv': nrm(ks[9], (L, D_MODEL, 2 * M_WIDTH), D_MODEL ** -0.5),
        'mq_norm_gain': gain(ks[10], (L, M_HEAD_DIM)),
        'mk_norm_gain': gain(ks[11], (L, M_HEAD_DIM)),
        'w_out': nrm(ks[12], (L, MIX_WIDTH, D_MODEL), MIX_WIDTH ** -0.5),
        'ffn_norm_gain': gain(ks[13], (L, D_MODEL)),
        'peer_w_q': nrm(ks[14], (L, D_MODEL, PEER_HEADS * PEER_KEY_DIM), D_MODEL ** -0.5),
        'peer_sub_keys_1': nrm(ks[15], (L, N_KEYS, PEER_KEY_DIM // 2), (PEER_KEY_DIM // 2) ** -0.5),
        'peer_sub_keys_2': nrm(ks[16], (L, N_KEYS, PEER_KEY_DIM // 2), (PEER_KEY_DIM // 2) ** -0.5),
        'peer_u': nrm(ks[17], (L, N_EXPERTS, D_MODEL), D_MODEL ** -0.5),
        'peer_v': nrm(ks[18], (L, N_EXPERTS, D_MODEL), 0.5),
    }


def reference(x, mem, attn_norm_gain, w_in, q_norm_gain, k_norm_gain, pool_w, pool_scale,
              mem_norm_gain, w_mem_kv, mq_norm_gain, mk_norm_gain, w_out,
              ffn_norm_gain, peer_w_q, peer_sub_keys_1, peer_sub_keys_2, peer_u, peer_v):
    b, s, _ = x.shape
    m = mem.shape[1]
    pos = jnp.arange(s)
    for l in range(DEPTH):
        xn = rms_norm(x, attn_norm_gain[l])
        z = jnp.einsum('bsd,de->bse', xn, w_in[l])
        q, k, v, qi, ki, wi, pu, qm = split_columns(z, IN_SPLITS)

        q = apply_rope(rms_norm(q.reshape(b, s, A_HEADS, A_HEAD_DIM), q_norm_gain[l]), pos)
        k = apply_rope(rms_norm(k.reshape(b, s, A_HEADS, A_HEAD_DIM), k_norm_gain[l]), pos)
        v = v.reshape(b, s, A_HEADS, A_HEAD_DIM)
        qi = apply_rope(qi.reshape(b, s, IDX_HEADS, IDX_DIM), pos)
        ki = apply_rope(ki[:, :, None, :], pos)[:, :, 0, :]
        wi = wi * (IDX_HEADS ** -0.5)
        a_out = dsa_sparse_attention(q, k, v, qi, ki, wi, pos)

        p_out = multiscale_pool_mixer(pu, pool_w[l], pool_scale[l])

        mn = rms_norm(mem, mem_norm_gain[l])
        kvm = jnp.einsum('bmd,de->bme', mn, w_mem_kv[l])
        km, vm = split_columns(kvm, (M_WIDTH, M_WIDTH))
        qm = rms_norm(qm.reshape(b, s, M_HEADS, M_HEAD_DIM), mq_norm_gain[l])
        km = rms_norm(km.reshape(b, m, M_HEADS, M_HEAD_DIM), mk_norm_gain[l])
        vm = vm.reshape(b, m, M_HEADS, M_HEAD_DIM)
        m_out = memory_cross_attention(qm, km, vm)

        mixed = jnp.concatenate([a_out, p_out, m_out], axis=-1)
        x = x + jnp.einsum('bse,ed->bsd', mixed, w_out[l])

        hn = rms_norm(x, ffn_norm_gain[l])
        x = x + peer_ffn(hn, peer_w_q[l], peer_sub_keys_1[l], peer_sub_keys_2[l], peer_u[l], peer_v[l])
    return x
```

```python
import functools

import numpy as np
import jax
import jax.numpy as jnp
from jax import lax
from jax.experimental import pallas as pl
from jax.experimental.pallas import tpu as pltpu

F32 = jnp.float32
I32 = jnp.int32
MXU_DTYPE = jnp.bfloat16

LANES = 128
VMEM_LIMIT = 56 * 1024 * 1024

EPS = 1e-6
ROPE_THETA = 10000.0
CHUNK = 64
A_HEADS = 8
HEAD_DIM = 128
IDX_HEADS = 16
IDX_DIM = 64
TOP_K_MAX = 256
POOL_WINDOWS = (2, 4, 8, 16)
M_HEADS = 4
PEER_HEADS = 8
N_KEYS = 128
PEER_TOPK = 16
INT_MIN = -(2 ** 31)
NEG_BIG = -1e30


def _cparams(*sem):
    return pltpu.CompilerParams(dimension_semantics=sem, vmem_limit_bytes=VMEM_LIMIT)


def _rms(x, g):
    ms = jnp.mean(x * x, axis=-1, keepdims=True)
    return x * lax.rsqrt(ms + EPS) * g


def _dot(a, b):
    return jnp.dot(a, b, preferred_element_type=F32)


def _dot_nt(a, b):
    return lax.dot_general(a, b, (((1,), (1,)), ((), ())), preferred_element_type=F32)


def _lane_iota(shape):
    return lax.broadcasted_iota(I32, shape, len(shape) - 1)


def _qkv_kernel(x_ref, g_ref, w_ref, hg_ref, cos_ref, sin_ref, o_ref, xn_ref):
    j = pl.program_id(1)

    @pl.when(j == 0)
    def _():
        xn_ref[...] = _rms(x_ref[...], g_ref[...]).astype(xn_ref.dtype)

    z = _dot(xn_ref[...], w_ref[...])

    @pl.when(j < 2)
    def _():
        cosf = cos_ref[...]
        sinf = sin_ref[...]
        hg = hg_ref[0]
        for h in range(A_HEADS):
            sl = slice(h * HEAD_DIM, (h + 1) * HEAD_DIM)
            n = _rms(z[:, sl], hg)
            o_ref[:, sl] = (n * cosf + pltpu.roll(n, HEAD_DIM // 2, 1) * sinf).astype(o_ref.dtype)

    @pl.when(j == 2)
    def _():
        o_ref[...] = z.astype(o_ref.dtype)


def _rope64(xb, cos4, sin4, first_half):
    rot = jnp.where(first_half, pltpu.roll(xb, LANES - IDX_DIM // 2, 1), pltpu.roll(xb, IDX_DIM // 2, 1))
    return xb * cos4 + rot * sin4


def _split3(v):
    hi = v.astype(MXU_DTYPE)
    r1 = v - hi.astype(F32)
    mid = r1.astype(MXU_DTYPE)
    lo = (r1 - mid.astype(F32)).astype(MXU_DTYPE)
    return hi, mid, lo


def _idx_kernel(x_ref, g_ref, wq_ref, ws_ref, e_ref, cos_ref, sin_ref, qi_ref, kd_ref, w_ref):
    xn = _rms(x_ref[...], g_ref[...]).astype(MXU_DTYPE)
    zq = _dot(xn, wq_ref[...])
    zs = _dot(xn, ws_ref[...])
    w_ref[...] = zs
    hi, mid, lo = _split3(zs)
    e = e_ref[...]
    wexp = _dot(hi, e) + _dot(mid, e) + _dot(lo, e)
    cos4 = cos_ref[...]
    sin4 = sin_ref[...]
    lane = _lane_iota(zs.shape)
    first_half = (lane & (IDX_DIM // 2)) == 0
    scale = (IDX_DIM ** -0.5) * (IDX_HEADS ** -0.5)
    for c in range(zq.shape[1] // LANES):
        sl = slice(c * LANES, (c + 1) * LANES)
        y = _rope64(zq[:, sl], cos4, sin4, first_half)
        qi_ref[:, sl] = (y * (wexp[:, sl] * scale)).astype(qi_ref.dtype)
    yk = _rope64(zs, cos4, sin4, first_half)
    kd_ref[...] = jnp.where(lane < IDX_DIM, yk, pltpu.roll(yk, IDX_DIM, 1)).astype(kd_ref.dtype)


def _plain_kernel(x_ref, g_ref, w_ref, o_ref):
    xn = _rms(x_ref[...], g_ref[...]).astype(MXU_DTYPE)
    o_ref[...] = _dot(xn, w_ref[...]).astype(o_ref.dtype)


def _headnorm_kernel(x_ref, g_ref, w_ref, hg_ref, o_ref, *, heads):
    xn = _rms(x_ref[...], g_ref[...]).astype(MXU_DTYPE)
    z = _dot(xn, w_ref[...])
    hg = hg_ref[...]
    for h in range(heads):
        sl = slice(h * HEAD_DIM, (h + 1) * HEAD_DIM)
        o_ref[:, sl] = _rms(z[:, sl], hg).astype(o_ref.dtype)


def _memkv_kernel(x_ref, g_ref, w_ref, hg_ref, o_ref):
    j = pl.program_id(0)
    xn = _rms(x_ref[...], g_ref[...]).astype(MXU_DTYPE)
    z = _dot(xn, w_ref[...])

    @pl.when(j == 0)
    def _():
        hg = hg_ref[...]
        for h in range(M_HEADS):
            sl = slice(h * HEAD_DIM, (h + 1) * HEAD_DIM)
            o_ref[:, sl] = _rms(z[:, sl], hg).astype(o_ref.dtype)

    @pl.when(j == 1)
    def _():
        o_ref[...] = z.astype(o_ref.dtype)


def _select_kernel(qi_ref, w_ref, kd_ref, mask_ref, lhs_ref, lo_ref, hi_ref, key_ref, *, tq, tk, seq, top_k):
    qt = pl.program_id(1)
    t0 = qt * tq
    n_kt = (t0 + tq + tk - 1) // tk
    n_cb = tk // LANES
    hgrp = 4

    lane = _lane_iota((tq, LANES))
    for h in range(IDX_HEADS):
        pair = qi_ref[:, (h // 2) * LANES:(h // 2 + 1) * LANES]
        own = (lane < IDX_DIM) if h % 2 == 0 else (lane >= IDX_DIM)
        lhs_ref[h * tq:(h + 1) * tq, :] = jnp.where(own, pair, jnp.zeros_like(pair))
        wh = w_ref[:, IDX_DIM + h:IDX_DIM + h + 1]
        pos = jnp.broadcast_to(wh > 0, (tq, LANES))
        lo_ref[h] = jnp.where(pos, 0.0, -jnp.inf).astype(F32)
        hi_ref[h] = jnp.where(pos, jnp.inf, 0.0).astype(F32)

    row_lim = ((t0 + lax.broadcasted_iota(I32, (tq, LANES), 0)) // CHUNK + 1) * CHUNK

    def score_tile(kt, carry):
        k0 = pl.multiple_of(kt * tk, tk)
        kd = kd_ref[pl.ds(k0, tk), :]
        accs = [jnp.zeros((tq, LANES), F32) for _ in range(n_cb)]
        for g in range(IDX_HEADS // hgrp):
            raw = _dot_nt(lhs_ref[g * hgrp * tq:(g + 1) * hgrp * tq, :], kd)
            for hh in range(hgrp):
                h = g * hgrp + hh
                lo = lo_ref[h]
                hi = hi_ref[h]
                for c in range(n_cb):
                    r = raw[hh * tq:(hh + 1) * tq, c * LANES:(c + 1) * LANES]
                    accs[c] = accs[c] + jnp.minimum(jnp.maximum(r, lo), hi)
        for c in range(n_cb):
            bits = pltpu.bitcast(accs[c], I32)
            key = bits ^ ((bits >> 31) & 0x7FFFFFFF)
            col = k0 + c * LANES + lane
            key = jnp.where(col < row_lim, key, INT_MIN)
            key_ref[:, pl.ds(pl.multiple_of(k0 + c * LANES, LANES), LANES)] = key
        return carry

    lax.fori_loop(0, n_kt, score_tile, 0)

    def bit_step(i, prefix):
        cand_u = prefix | lax.shift_left(jnp.int32(1), (31 - i).astype(I32))
        cand = jnp.broadcast_to(cand_u ^ INT_MIN, (tq, LANES))

        def count_tile(kt, cnt):
            k0 = pl.multiple_of(kt * tk, tk)
            for c in range(n_cb):
                key = key_ref[:, pl.ds(pl.multiple_of(k0 + c * LANES, LANES), LANES)]
                cnt = cnt + (key >= cand).astype(I32)
            return cnt

        cnt = lax.fori_loop(0, n_kt, count_tile, jnp.zeros((tq, LANES), I32))
        total = jnp.sum(cnt, axis=1, keepdims=True)
        return jnp.where(total >= top_k, cand_u, prefix)

    prefix = lax.fori_loop(0, 32, bit_step, jnp.zeros((tq, 1), I32))
    tau = jnp.broadcast_to(prefix ^ INT_MIN, (tq, LANES))

    def write_tile(kt, carry):
        k0 = pl.multiple_of(kt * tk, tk)
        for c in range(n_cb):
            ds = pl.ds(pl.multiple_of(k0 + c * LANES, LANES), LANES)
            key = key_ref[:, ds]
            sel = jnp.logical_and(key >= tau, key > INT_MIN)
            mask_ref[:, ds] = sel.astype(mask_ref.dtype)
        return carry

    lax.fori_loop(0, n_kt, write_tile, 0)

    def zero_tile(kt, carry):
        k0 = pl.multiple_of(kt * tk, tk)
        mask_ref[:, pl.ds(k0, tk)] = jnp.zeros((tq, tk), mask_ref.dtype)
        return carry

    lax.fori_loop(n_kt, seq // tk, zero_tile, 0)


def _attn_kernel(qt_tab, kt_tab, q_ref, k_ref, v_ref, mask_ref, o_ref, m_ref, l_ref, acc_ref, *, tq, tk):
    step = pl.program_id(1)
    qt = qt_tab[step]
    kt = kt_tab[step]
    last_kt = ((qt + 1) * tq - 1) // tk

    @pl.when(kt == 0)
    def _():
        m_ref[...] = jnp.full(m_ref.shape, NEG_BIG, F32)
        l_ref[...] = jnp.zeros(l_ref.shape, F32)
        acc_ref[...] = jnp.zeros(acc_ref.shape, F32)

    bias = jnp.where(mask_ref[...].astype(I32) != 0, 0.0, -jnp.inf).astype(F32)
    for h in range(A_HEADS):
        sl = slice(h * HEAD_DIM, (h + 1) * HEAD_DIM)
        s = _dot_nt(q_ref[:, sl], k_ref[:, sl]) + bias
        m_prev = m_ref[h]
        m_new = jnp.maximum(m_prev, jnp.max(s, axis=1, keepdims=True))
        alpha = jnp.exp(m_prev - m_new)
        p = jnp.exp(s - m_new)
        l_ref[h] = alpha * l_ref[h] + jnp.sum(p, axis=1, keepdims=True)
        acc_ref[:, sl] = alpha * acc_ref[:, sl] + _dot(p.astype(MXU_DTYPE), v_ref[:, sl])
        m_ref[h] = m_new

    @pl.when(kt == last_kt)
    def _():
        for h in range(A_HEADS):
            sl = slice(h * HEAD_DIM, (h + 1) * HEAD_DIM)
            o_ref[:, sl] = (acc_ref[:, sl] / l_ref[h]).astype(o_ref.dtype)


def _mix_kernel(x_ref, a_ref, pu_ref, halo_ref, qm_ref, kvm_ref, pw_ref, ps_ref, wo_ref, fg_ref,
                h_ref, hn_ref, *, tm, seq):
    i = pl.program_id(0)
    tiles_per_seq = seq // tm
    pos0 = (i % tiles_per_seq) * tm
    halo_rows = halo_ref.shape[0]
    a_w = a_ref.shape[1]
    p_w = pu_ref.shape[1]
    grp = p_w // len(POOL_WINDOWS)

    halo = jnp.where(pos0 > 0, halo_ref[...], 0.0)
    ext = jnp.concatenate([halo, pu_ref[...]], axis=0)
    pos = pos0 + lax.broadcasted_iota(I32, (tm, grp), 0)
    out = x_ref[...] + _dot(a_ref[...], wo_ref[0:a_w, :])
    run = ext
    width = 1
    for g, w in enumerate(POOL_WINDOWS):
        while width < w:
            run = run + pltpu.roll(run, width, 0)
            width *= 2
        sl = slice(g * grp, (g + 1) * grp)
        wsum = run[halo_rows:, sl]
        cnt = jnp.minimum(pos + 1, w).astype(F32)
        d = wsum / cnt - pu_ref[:, sl]
        pg = _dot(d.astype(MXU_DTYPE), pw_ref[g]) * ps_ref[:, sl]
        out = out + _dot(pg.astype(MXU_DTYPE), wo_ref[a_w + g * grp:a_w + (g + 1) * grp, :])

    m_w = qm_ref.shape[1]
    for h in range(M_HEADS):
        sl = slice(h * HEAD_DIM, (h + 1) * HEAD_DIM)
        s = _dot_nt(qm_ref[:, sl], kvm_ref[:, sl])
        p = jnp.exp(s - jnp.max(s, axis=1, keepdims=True))
        o = _dot(p.astype(MXU_DTYPE), kvm_ref[:, m_w + h * HEAD_DIM:m_w + (h + 1) * HEAD_DIM])
        o = o / jnp.sum(p, axis=1, keepdims=True)
        r0 = a_w + p_w + h * HEAD_DIM
        out = out + _dot(o.astype(MXU_DTYPE), wo_ref[r0:r0 + HEAD_DIM, :])

    h_ref[...] = out
    hn_ref[...] = _rms(out, fg_ref[...]).astype(hn_ref.dtype)


def _top16_rows(s, row):
    rank = jnp.full(s.shape, float(PEER_TOPK), F32)
    big = jnp.int32(s.shape[0])
    vals = []
    for k in range(PEER_TOPK):
        m = jnp.max(s, axis=0, keepdims=True)
        first = jnp.min(jnp.where(s == m, row, big), axis=0, keepdims=True)
        hit = row == first
        rank = jnp.where(hit, float(k), rank)
        s = jnp.where(hit, -jnp.inf, s)
        vals.append(m)
    return jnp.concatenate(vals, axis=0), rank


def _route_kernel(hn_ref, wq_ref, k1_ref, k2_ref, r2_ref, e2_ref, n_ref, c_ref, *, tm):
    q = _dot(hn_ref[...], wq_ref[...]).astype(MXU_DTYPE)
    row = lax.broadcasted_iota(I32, (N_KEYS, tm), 0)
    sub = lax.broadcasted_iota(I32, (8, tm), 0)
    half = N_KEYS
    for h in range(PEER_HEADS):
        s1 = _dot_nt(k1_ref[...], q[:, h * 2 * half:h * 2 * half + half])
        s2 = _dot_nt(k2_ref[...], q[:, h * 2 * half + half:(h + 1) * 2 * half])
        v1, rank1 = _top16_rows(s1, row)
        v2, rank2 = _top16_rows(s2, row)

        blocks, flats = [], []
        for half_a in range(2):
            blocks.append(v1[half_a * 8:(half_a + 1) * 8] + v2[0:1])
            flats.append((sub + half_a * 8) * PEER_TOPK)
        for b in range(1, 8):
            blk = v1[0:8] + v2[b:b + 1]
            blocks.append(jnp.where(sub < PEER_TOPK // (b + 1), blk, -jnp.inf))
            flats.append(sub * PEER_TOPK + b)
        blocks.append(v1[0:1] + v2[8:16])
        flats.append(sub + 8)
        cand = jnp.concatenate(blocks, axis=0)
        flat = jnp.concatenate(flats, axis=0)
        cand0 = cand
        sel = jnp.zeros(cand.shape, jnp.bool_)
        for _ in range(PEER_TOPK):
            m = jnp.max(cand, axis=0, keepdims=True)
            first = jnp.min(jnp.where(cand == m, flat, PEER_TOPK * PEER_TOPK), axis=0, keepdims=True)
            hit = flat == first
            sel = jnp.logical_or(sel, hit)
            cand = jnp.where(hit, -jnp.inf, cand)
        top = v1[0:1] + v2[0:1]
        z = jnp.sum(jnp.where(sel, jnp.exp(cand0 - top), 0.0), axis=0, keepdims=True)
        self32 = sel.astype(F32)
        nb_lo = self32[0:8]
        for blk in range(2, 9):
            nb_lo = nb_lo + self32[blk * 8:(blk + 1) * 8]
        tail = jnp.sum(self32[72:80], axis=0, keepdims=True)
        nb_lo = nb_lo + jnp.where(sub == 0, tail, 0.0)
        nb = jnp.concatenate([nb_lo, self32[8:16]], axis=0)

        n_dense = jnp.zeros((N_KEYS, tm), F32)
        for a in range(PEER_TOPK):
            n_dense = jnp.where(rank1 == float(a), nb[a:a + 1], n_dense)
        r2_ref[h] = rank2
        e2_ref[h] = jnp.exp(s2 - v2[0:1])
        n_ref[h] = n_dense
        c_ref[h] = jnp.exp(s1 - v1[0:1]) / z


def _gelu(x):
    return 0.5 * x * (1.0 + lax.erf(x * (2.0 ** -0.5)))


def _peer_kernel(hn_ref, u_ref, vt_ref, r2_ref, e2_ref, n_ref, c_ref, h_ref, o_ref, acc_ref, at_ref, *, tb, eb):
    ei = pl.program_id(1)

    @pl.when(ei == 0)
    def _():
        acc_ref[...] = jnp.zeros(acc_ref.shape, F32)

    ht = _dot_nt(u_ref[...], hn_ref[...])
    ipb = eb // N_KEYS
    sub = 8
    base = pl.multiple_of((ei * ipb) // sub * sub, sub)
    off = (ei * ipb) % sub
    for il in range(ipb):
        for tcol in range(tb // LANES):
            cs = slice(tcol * LANES, (tcol + 1) * LANES)
            gate = jnp.zeros((N_KEYS, LANES), F32)
            for h in range(PEER_HEADS):
                n8 = n_ref[h, pl.ds(base, sub), cs]
                c8 = c_ref[h, pl.ds(base, sub), cs]
                nrow = n8[il:il + 1]
                crow = c8[il:il + 1]
                for o in range(ipb, sub, ipb):
                    nrow = jnp.where(off == o, n8[o + il:o + il + 1], nrow)
                    crow = jnp.where(off == o, c8[o + il:o + il + 1], crow)
                gate = gate + jnp.where(r2_ref[h, :, cs] < nrow, e2_ref[h, :, cs], 0.0) * crow
            a = gate * _gelu(ht[il * N_KEYS:(il + 1) * N_KEYS, cs])
            at_ref[il * N_KEYS:(il + 1) * N_KEYS, cs] = a.astype(at_ref.dtype)
    acc_ref[...] += _dot(vt_ref[...], at_ref[...])

    @pl.when(ei == pl.num_programs(1) - 1)
    def _():
        o_ref[...] = h_ref[...] + acc_ref[...].T


def _rope_tables(seq):
    pos = jnp.arange(seq, dtype=F32)[:, None]

    def table(half):
        inv = ROPE_THETA ** (-jnp.arange(half, dtype=F32) / half)
        ang = pos * inv[None, :]
        return jnp.cos(ang), jnp.sin(ang)

    c64, s64 = table(HEAD_DIM // 2)
    c32, s32 = table(IDX_DIM // 2)
    cos_a = jnp.concatenate([c64, c64], axis=1)
    sin_a = jnp.concatenate([-s64, s64], axis=1)
    cos_i = jnp.concatenate([c32, c32, c32, c32], axis=1)
    sin_i = jnp.concatenate([-s32, s32, -s32, s32], axis=1)
    return cos_a, sin_a, cos_i, sin_i


def _attn_steps(seq, tq, tk):
    qts, kts = [], []
    for qt in range(seq // tq):
        for kt in range(((qt + 1) * tq - 1) // tk + 1):
            qts.append(qt)
            kts.append(kt)
    return np.asarray(qts, np.int32), np.asarray(kts, np.int32)


def _full(shape):
    return pl.BlockSpec(shape, lambda *_: (0,) * len(shape))


def _layer(x, mem, attn_g, w_in, q_g, k_g, pool_w, pool_s, mem_g, w_mem_kv, mq_g, mk_g, w_out,
           ffn_g, peer_wq, sk1, sk2, peer_u, peer_v):
    b, seq, d = x.shape
    t = b * seq
    mem_len = mem.shape[1]
    a_w = A_HEADS * HEAD_DIM
    i_w = IDX_HEADS * IDX_DIM
    p_w = pool_w.shape[0] * pool_w.shape[1]
    m_w = M_HEADS * HEAD_DIM
    top_k = min(TOP_K_MAX, seq // 4)
    md = MXU_DTYPE

    x2 = x.reshape(t, d)
    row = lambda v: v.reshape(1, -1).astype(F32)
    cos_a, sin_a, cos_i, sin_i = _rope_tables(seq)

    o = 0
    w_qkv = w_in[:, o:o + 3 * a_w].astype(md); o += 3 * a_w
    w_qi = w_in[:, o:o + i_w].astype(md); o += i_w
    w_small = jnp.pad(w_in[:, o:o + IDX_DIM + IDX_HEADS], ((0, 0), (0, LANES - IDX_DIM - IDX_HEADS))).astype(md)
    o += IDX_DIM + IDX_HEADS
    w_pu = w_in[:, o:o + p_w].astype(md); o += p_w
    w_qm = w_in[:, o:o + m_w].astype(md)

    tm = min(1024, seq)
    n_rt = t // tm
    pos_blk = lambda i: i % (seq // tm)
    hg = jnp.stack([row(q_g) * (HEAD_DIM ** -0.5), row(k_g), jnp.ones((1, HEAD_DIM), F32)])
    qkv = pl.pallas_call(
        _qkv_kernel,
        grid=(n_rt, 3),
        in_specs=[pl.BlockSpec((tm, d), lambda i, j: (i, 0)),
                  pl.BlockSpec((1, d), lambda i, j: (0, 0)),
                  pl.BlockSpec((d, a_w), lambda i, j: (0, j)),
                  pl.BlockSpec((1, 1, HEAD_DIM), lambda i, j: (j, 0, 0)),
                  pl.BlockSpec((tm, HEAD_DIM), lambda i, j: (pos_blk(i), 0)),
                  pl.BlockSpec((tm, HEAD_DIM), lambda i, j: (pos_blk(i), 0))],
        out_specs=pl.BlockSpec((tm, a_w), lambda i, j: (i, j)),
        out_shape=jax.ShapeDtypeStruct((t, 3 * a_w), md),
        scratch_shapes=[pltpu.VMEM((tm, d), md)],
        compiler_params=_cparams("parallel", "arbitrary"),
        name="qkv_proj",
    )(x2, row(attn_g), w_qkv, hg, cos_a, sin_a)

    expand = np.zeros((LANES, i_w), np.float32)
    for h in range(IDX_HEADS):
        expand[IDX_DIM + h, h * IDX_DIM:(h + 1) * IDX_DIM] = 1.0
    tmi = min(512, seq)
    qi, kd, wv = pl.pallas_call(
        _idx_kernel,
        grid=(t // tmi,),
        in_specs=[pl.BlockSpec((tmi, d), lambda i: (i, 0)),
                  _full((1, d)), _full((d, i_w)), _full((d, LANES)), _full((LANES, i_w)),
                  pl.BlockSpec((tmi, LANES), lambda i: (i % (seq // tmi), 0)),
                  pl.BlockSpec((tmi, LANES), lambda i: (i % (seq // tmi), 0))],
        out_specs=[pl.BlockSpec((tmi, i_w), lambda i: (i, 0)),
                   pl.BlockSpec((tmi, LANES), lambda i: (i, 0)),
                   pl.BlockSpec((tmi, LANES), lambda i: (i, 0))],
        out_shape=[jax.ShapeDtypeStruct((t, i_w), md),
                   jax.ShapeDtypeStruct((t, LANES), md),
                   jax.ShapeDtypeStruct((t, LANES), F32)],
        compiler_params=_cparams("parallel"),
        name="idx_proj",
    )(x2, row(attn_g), w_qi, w_small, jnp.asarray(expand, md), cos_i, sin_i)

    pu = pl.pallas_call(
        _plain_kernel,
        grid=(t // tmi,),
        in_specs=[pl.BlockSpec((tmi, d), lambda i: (i, 0)), _full((1, d)), _full((d, p_w))],
        out_specs=pl.BlockSpec((tmi, p_w), lambda i: (i, 0)),
        out_shape=jax.ShapeDtypeStruct((t, p_w), F32),
        compiler_params=_cparams("parallel"),
        name="pool_proj",
    )(x2, row(attn_g), w_pu)

    qm = pl.pallas_call(
        functools.partial(_headnorm_kernel, heads=M_HEADS),
        grid=(t // tmi,),
        in_specs=[pl.BlockSpec((tmi, d), lambda i: (i, 0)), _full((1, d)), _full((d, m_w)), _full((1, HEAD_DIM))],
        out_specs=pl.BlockSpec((tmi, m_w), lambda i: (i, 0)),
        out_shape=jax.ShapeDtypeStruct((t, m_w), md),
        compiler_params=_cparams("parallel"),
        name="memq_proj",
    )(x2, row(attn_g), w_qm, row(mq_g) * (HEAD_DIM ** -0.5))

    tmem = b * mem_len
    kvm = pl.pallas_call(
        _memkv_kernel,
        grid=(2,),
        in_specs=[_full((tmem, d)), _full((1, d)),
                  pl.BlockSpec((d, m_w), lambda j: (0, j)), _full((1, HEAD_DIM))],
        out_specs=pl.BlockSpec((tmem, m_w), lambda j: (0, j)),
        out_shape=jax.ShapeDtypeStruct((tmem, 2 * m_w), md),
        compiler_params=_cparams("arbitrary"),
        name="memkv_proj",
    )(mem.reshape(tmem, d), row(mem_g), w_mem_kv.astype(md), row(mk_g))

    tq, tks = 256, 512
    mask = pl.pallas_call(
        functools.partial(_select_kernel, tq=tq, tk=tks, seq=seq, top_k=top_k),
        grid=(b, seq // tq),
        in_specs=[pl.BlockSpec((None, tq, i_w), lambda bi, qt: (bi, qt, 0)),
                  pl.BlockSpec((None, tq, LANES), lambda bi, qt: (bi, qt, 0)),
                  pl.BlockSpec((None, seq, LANES), lambda bi, qt: (bi, 0, 0))],
        out_specs=pl.BlockSpec((None, tq, seq), lambda bi, qt: (bi, qt, 0)),
        out_shape=jax.ShapeDtypeStruct((b, seq, seq), jnp.int8),
        scratch_shapes=[pltpu.VMEM((IDX_HEADS * tq, LANES), md),
                        pltpu.VMEM((IDX_HEADS, tq, LANES), F32),
                        pltpu.VMEM((IDX_HEADS, tq, LANES), F32),
                        pltpu.VMEM((tq, seq), I32)],
        compiler_params=_cparams("parallel", "arbitrary"),
        name="dsa_select",
    )(qi.reshape(b, seq, i_w), wv.reshape(b, seq, LANES), kd.reshape(b, seq, LANES))

    tka = 512
    qts, kts = _attn_steps(seq, tq, tka)
    qkv3 = qkv.reshape(b, seq, 3 * a_w)
    a_out = pl.pallas_call(
        functools.partial(_attn_kernel, tq=tq, tk=tka),
        grid_spec=pltpu.PrefetchScalarGridSpec(
            num_scalar_prefetch=2,
            grid=(b, len(qts)),
            in_specs=[pl.BlockSpec((None, tq, a_w), lambda bi, s, qt, kt: (bi, qt[s], 0)),
                      pl.BlockSpec((None, tka, a_w), lambda bi, s, qt, kt: (bi, kt[s], 1)),
                      pl.BlockSpec((None, tka, a_w), lambda bi, s, qt, kt: (bi, kt[s], 2)),
                      pl.BlockSpec((None, tq, tka), lambda bi, s, qt, kt: (bi, qt[s], kt[s]))],
            out_specs=pl.BlockSpec((None, tq, a_w), lambda bi, s, qt, kt: (bi, qt[s], 0)),
            scratch_shapes=[pltpu.VMEM((A_HEADS, tq, 1), F32),
                            pltpu.VMEM((A_HEADS, tq, 1), F32),
                            pltpu.VMEM((tq, a_w), F32)]),
        out_shape=jax.ShapeDtypeStruct((b, seq, a_w), md),
        compiler_params=_cparams("parallel", "arbitrary"),
        name="dsa_attn",
    )(jnp.asarray(qts), jnp.asarray(kts), qkv3, qkv3, qkv3, mask)

    tmx = min(512, seq)
    halo_rows = POOL_WINDOWS[-1]
    hsub = tmx // halo_rows
    h_res, hn = pl.pallas_call(
        functools.partial(_mix_kernel, tm=tmx, seq=seq),
        grid=(t // tmx,),
        in_specs=[pl.BlockSpec((tmx, d), lambda i: (i, 0)),
                  pl.BlockSpec((tmx, a_w), lambda i: (i, 0)),
                  pl.BlockSpec((tmx, p_w), lambda i: (i, 0)),
                  pl.BlockSpec((halo_rows, p_w), lambda i: (jnp.maximum(i * hsub - 1, 0), 0)),
                  pl.BlockSpec((tmx, m_w), lambda i: (i, 0)),
                  pl.BlockSpec((mem_len, 2 * m_w), lambda i: (i // (seq // tmx), 0)),
                  _full(pool_w.shape), _full((1, p_w)), _full((d, d)), _full((1, d))],
        out_specs=[pl.BlockSpec((tmx, d), lambda i: (i, 0)),
                   pl.BlockSpec((tmx, d), lambda i: (i, 0))],
        out_shape=[jax.ShapeDtypeStruct((t, d), F32), jax.ShapeDtypeStruct((t, d), md)],
        compiler_params=_cparams("parallel"),
        name="mix_out",
    )(x2, a_out.reshape(t, a_w), pu, pu, qm, kvm, pool_w.astype(md), row(pool_s), w_out.astype(md), row(ffn_g))

    tmr = 256
    rshape = jax.ShapeDtypeStruct((PEER_HEADS, N_KEYS, t), F32)
    rspec = pl.BlockSpec((PEER_HEADS, N_KEYS, tmr), lambda i: (0, 0, i))
    r2, e2, nsel, csel = pl.pallas_call(
        functools.partial(_route_kernel, tm=tmr),
        grid=(t // tmr,),
        in_specs=[pl.BlockSpec((tmr, d), lambda i: (i, 0)),
                  _full(peer_wq.shape), _full(sk1.shape), _full(sk2.shape)],
        out_specs=[rspec, rspec, rspec, rspec],
        out_shape=[rshape, rshape, rshape, rshape],
        compiler_params=_cparams("parallel"),
        name="peer_route",
    )(hn, peer_wq.astype(md), sk1.astype(md), sk2.astype(md))

    tb, eb = 512, 512
    n_exp = peer_u.shape[0]
    gspec = pl.BlockSpec((PEER_HEADS, N_KEYS, tb), lambda ti, ei: (0, 0, ti))
    out = pl.pallas_call(
        functools.partial(_peer_kernel, tb=tb, eb=eb),
        grid=(t // tb, n_exp // eb),
        in_specs=[pl.BlockSpec((tb, d), lambda ti, ei: (ti, 0)),
                  pl.BlockSpec((eb, d), lambda ti, ei: (ei, 0)),
                  pl.BlockSpec((d, eb), lambda ti, ei: (0, ei)),
                  gspec, gspec, gspec, gspec,
                  pl.BlockSpec((tb, d), lambda ti, ei: (ti, 0))],
        out_specs=pl.BlockSpec((tb, d), lambda ti, ei: (ti, 0)),
        out_shape=jax.ShapeDtypeStruct((t, d), F32),
        scratch_shapes=[pltpu.VMEM((d, tb), F32), pltpu.VMEM((eb, tb), md)],
        compiler_params=_cparams("parallel", "arbitrary"),
        name="peer_experts",
    )(hn, peer_u.astype(md), peer_v.T.astype(md), r2, e2, nsel, csel, h_res)
    return out.reshape(b, seq, d)


def kernel(x, mem, attn_norm_gain, w_in, q_norm_gain, k_norm_gain, pool_w, pool_scale, mem_norm_gain,
           w_mem_kv, mq_norm_gain, mk_norm_gain, w_out, ffn_norm_gain, peer_w_q, peer_sub_keys_1,
           peer_sub_keys_2, peer_u, peer_v):
    for l in range(attn_norm_gain.shape[0]):
        x = _layer(x, mem, attn_norm_gain[l], w_in[l], q_norm_gain[l], k_norm_gain[l], pool_w[l],
                   pool_scale[l], mem_norm_gain[l], w_mem_kv[l], mq_norm_gain[l], mk_norm_gain[l],
                   w_out[l], ffn_norm_gain[l], peer_w_q[l], peer_sub_keys_1[l], peer_sub_keys_2[l],
                   peer_u[l], peer_v[l])
    return x
```

```python
import functools

import numpy as np
import jax
import jax.numpy as jnp
from jax import lax
from jax.experimental import pallas as pl
from jax.experimental.pallas import tpu as pltpu

F32 = jnp.float32
I32 = jnp.int32
MXU_DTYPE = jnp.bfloat16

LANES = 128
VMEM_LIMIT = 56 * 1024 * 1024

EPS = 1e-6
ROPE_THETA = 10000.0
CHUNK = 64
A_HEADS = 8
HEAD_DIM = 128
IDX_HEADS = 16
IDX_DIM = 64
TOP_K_MAX = 256
POOL_WINDOWS = (2, 4, 8, 16)
M_HEADS = 4
PEER_HEADS = 8
N_KEYS = 128
PEER_TOPK = 16
INT_MIN = -(2 ** 31)
NEG_BIG = -1e30


def _cparams(*sem):
    return pltpu.CompilerParams(dimension_semantics=sem, vmem_limit_bytes=VMEM_LIMIT)


def _rms(x, g):
    ms = jnp.mean(x * x, axis=-1, keepdims=True)
    return x * lax.rsqrt(ms + EPS) * g


def _dot(a, b):
    return jnp.dot(a, b, preferred_element_type=F32)


def _dot_nt(a, b):
    return lax.dot_general(a, b, (((1,), (1,)), ((), ())), preferred_element_type=F32)


def _lane_iota(shape):
    return lax.broadcasted_iota(I32, shape, len(shape) - 1)


def _qkv_kernel(x_ref, g_ref, w_ref, hg_ref, cos_ref, sin_ref, o_ref, xn_ref):
    j = pl.program_id(1)

    @pl.when(j == 0)
    def _():
        xn_ref[...] = _rms(x_ref[...], g_ref[...]).astype(xn_ref.dtype)

    z = _dot(xn_ref[...], w_ref[...])

    @pl.when(j < 2)
    def _():
        cosf = cos_ref[...]
        sinf = sin_ref[...]
        hg = hg_ref[0]
        for h in range(A_HEADS):
            sl = slice(h * HEAD_DIM, (h + 1) * HEAD_DIM)
            n = _rms(z[:, sl], hg)
            o_ref[:, sl] = (n * cosf + pltpu.roll(n, HEAD_DIM // 2, 1) * sinf).astype(o_ref.dtype)

    @pl.when(j == 2)
    def _():
        o_ref[...] = z.astype(o_ref.dtype)


def _rope64(xb, cos4, sin4, first_half):
    rot = jnp.where(first_half, pltpu.roll(xb, LANES - IDX_DIM // 2, 1), pltpu.roll(xb, IDX_DIM // 2, 1))
    return xb * cos4 + rot * sin4


def _split3(v):
    hi = v.astype(MXU_DTYPE)
    r1 = v - hi.astype(F32)
    mid = r1.astype(MXU_DTYPE)
    lo = (r1 - mid.astype(F32)).astype(MXU_DTYPE)
    return hi, mid, lo


def _idx_kernel(x_ref, g_ref, wq_ref, ws_ref, e_ref, cos_ref, sin_ref, qi_ref, kd_ref, w_ref):
    xn = _rms(x_ref[...], g_ref[...]).astype(MXU_DTYPE)
    zq = _dot(xn, wq_ref[...])
    zs = _dot(xn, ws_ref[...])
    w_ref[...] = zs
    hi, mid, lo = _split3(zs)
    e = e_ref[...]
    wexp = _dot(hi, e) + _dot(mid, e) + _dot(lo, e)
    cos4 = cos_ref[...]
    sin4 = sin_ref[...]
    lane = _lane_iota(zs.shape)
    first_half = (lane & (IDX_DIM // 2)) == 0
    scale = (IDX_DIM ** -0.5) * (IDX_HEADS ** -0.5)
    for c in range(zq.shape[1] // LANES):
        sl = slice(c * LANES, (c + 1) * LANES)
        y = _rope64(zq[:, sl], cos4, sin4, first_half)
        qi_ref[:, sl] = (y * (wexp[:, sl] * scale)).astype(qi_ref.dtype)
    yk = _rope64(zs, cos4, sin4, first_half)
    kd_ref[...] = jnp.where(lane < IDX_DIM, yk, pltpu.roll(yk, IDX_DIM, 1)).astype(kd_ref.dtype)


def _plain_kernel(x_ref, g_ref, w_ref, o_ref):
    xn = _rms(x_ref[...], g_ref[...]).astype(MXU_DTYPE)
    o_ref[...] = _dot(xn, w_ref[...]).astype(o_ref.dtype)


def _headnorm_kernel(x_ref, g_ref, w_ref, hg_ref, o_ref, *, heads):
    xn = _rms(x_ref[...], g_ref[...]).astype(MXU_DTYPE)
    z = _dot(xn, w_ref[...])
    hg = hg_ref[...]
    for h in range(heads):
        sl = slice(h * HEAD_DIM, (h + 1) * HEAD_DIM)
        o_ref[:, sl] = _rms(z[:, sl], hg).astype(o_ref.dtype)


def _memkv_kernel(x_ref, g_ref, w_ref, hg_ref, o_ref):
    j = pl.program_id(0)
    xn = _rms(x_ref[...], g_ref[...]).astype(MXU_DTYPE)
    z = _dot(xn, w_ref[...])

    @pl.when(j == 0)
    def _():
        hg = hg_ref[...]
        for h in range(M_HEADS):
            sl = slice(h * HEAD_DIM, (h + 1) * HEAD_DIM)
            o_ref[:, sl] = _rms(z[:, sl], hg).astype(o_ref.dtype)

    @pl.when(j == 1)
    def _():
        o_ref[...] = z.astype(o_ref.dtype)


def _select_kernel(qi_ref, w_ref, kd_ref, mask_ref, lhs_ref, lo_ref, hi_ref, khi_ref, klo_ref, *, tq, tk, seq, top_k):
    qt = pl.program_id(1)
    t0 = qt * tq
    n_kt = (t0 + tq + tk - 1) // tk
    n_cb = tk // LANES
    hgrp = 4
    i16 = jnp.int16
    half_min = -(2 ** 15)

    lane = _lane_iota((tq, LANES))
    for h in range(IDX_HEADS):
        pair = qi_ref[:, (h // 2) * LANES:(h // 2 + 1) * LANES]
        own = (lane < IDX_DIM) if h % 2 == 0 else (lane >= IDX_DIM)
        lhs_ref[h * tq:(h + 1) * tq, :] = jnp.where(own, pair, jnp.zeros_like(pair))
        wh = w_ref[:, IDX_DIM + h:IDX_DIM + h + 1]
        pos = jnp.broadcast_to(wh > 0, (tq, LANES))
        lo_ref[h] = jnp.where(pos, 0.0, -jnp.inf).astype(F32)
        hi_ref[h] = jnp.where(pos, jnp.inf, 0.0).astype(F32)

    row_lim = ((t0 + lax.broadcasted_iota(I32, (tq, LANES), 0)) // CHUNK + 1) * CHUNK

    def col_ds(kt, c):
        return pl.ds(pl.multiple_of(kt * tk + c * LANES, LANES), LANES)

    def score_tile(kt, carry):
        k0 = pl.multiple_of(kt * tk, tk)
        kd = kd_ref[pl.ds(k0, tk), :]
        accs = [jnp.zeros((tq, LANES), F32) for _ in range(n_cb)]
        for g in range(IDX_HEADS // hgrp):
            raw = _dot_nt(lhs_ref[g * hgrp * tq:(g + 1) * hgrp * tq, :], kd)
            for hh in range(hgrp):
                h = g * hgrp + hh
                lo = lo_ref[h]
                hi = hi_ref[h]
                for c in range(n_cb):
                    r = raw[hh * tq:(hh + 1) * tq, c * LANES:(c + 1) * LANES]
                    accs[c] = accs[c] + jnp.minimum(jnp.maximum(r, lo), hi)
        for c in range(n_cb):
            bits = pltpu.bitcast(accs[c], I32)
            key = bits ^ ((bits >> 31) & 0x7FFFFFFF)
            col = k0 + c * LANES + lane
            key = jnp.where(col < row_lim, key, INT_MIN)
            khi_ref[:, col_ds(kt, c)] = (key >> 16).astype(i16)
            klo_ref[:, col_ds(kt, c)] = ((key & 0xFFFF) + half_min).astype(i16)
        return carry

    lax.fori_loop(0, n_kt, score_tile, 0)

    one16 = jnp.ones((tq, LANES), i16)
    zero16 = jnp.zeros((tq, LANES), i16)

    def kth_largest(ref, want):
        def bit_step(i, carry):
            prefix, above = carry
            cand_u = prefix | lax.shift_left(jnp.int32(1), (15 - i).astype(I32))
            cand = jnp.broadcast_to(cand_u + half_min, (tq, LANES)).astype(i16)

            def count_tile(kt, cnt):
                for c in range(n_cb):
                    cnt = cnt + jnp.where(ref[:, col_ds(kt, c)] >= cand, one16, zero16)
                return cnt

            cnt = lax.fori_loop(0, n_kt, count_tile, zero16)
            total = jnp.sum(cnt.astype(I32), axis=1, keepdims=True)
            keep = total >= want
            return jnp.where(keep, cand_u, prefix), jnp.where(keep, above, total)

        zero = jnp.zeros((tq, 1), I32)
        prefix, above = lax.fori_loop(0, 16, bit_step, (zero, zero))
        return prefix + half_min, above

    tau_hi, above = kth_largest(khi_ref, top_k)
    tau_hi_b = jnp.broadcast_to(tau_hi, (tq, LANES)).astype(i16)
    min16 = jnp.full((tq, LANES), half_min, i16)

    def bucket_tile(kt, carry):
        for c in range(n_cb):
            ds = col_ds(kt, c)
            klo_ref[:, ds] = jnp.where(khi_ref[:, ds] == tau_hi_b, klo_ref[:, ds], min16)
        return carry

    lax.fori_loop(0, n_kt, bucket_tile, 0)
    tau_lo, _ = kth_largest(klo_ref, top_k - above)
    tau_lo = jnp.where(tau_hi == half_min, half_min + 1, tau_lo)
    tau_lo_b = jnp.broadcast_to(tau_lo, (tq, LANES)).astype(i16)

    def write_tile(kt, carry):
        for c in range(n_cb):
            ds = col_ds(kt, c)
            khi = khi_ref[:, ds]
            sel = jnp.logical_or(khi > tau_hi_b, jnp.logical_and(khi == tau_hi_b, klo_ref[:, ds] >= tau_lo_b))
            mask_ref[:, ds] = sel.astype(mask_ref.dtype)
        return carry

    lax.fori_loop(0, n_kt, write_tile, 0)

    def zero_tile(kt, carry):
        k0 = pl.multiple_of(kt * tk, tk)
        mask_ref[:, pl.ds(k0, tk)] = jnp.zeros((tq, tk), mask_ref.dtype)
        return carry

    lax.fori_loop(n_kt, seq // tk, zero_tile, 0)


def _attn_kernel(qt_tab, kt_tab, q_ref, k_ref, v_ref, mask_ref, o_ref, m_ref, l_ref, acc_ref, *, tq, tk):
    step = pl.program_id(1)
    qt = qt_tab[step]
    kt = kt_tab[step]
    last_kt = ((qt + 1) * tq - 1) // tk

    @pl.when(kt == 0)
    def _():
        m_ref[...] = jnp.full(m_ref.shape, NEG_BIG, F32)
        l_ref[...] = jnp.zeros(l_ref.shape, F32)
        acc_ref[...] = jnp.zeros(acc_ref.shape, F32)

    bias = jnp.where(mask_ref[...].astype(I32) != 0, 0.0, -jnp.inf).astype(F32)
    for h in range(A_HEADS):
        sl = slice(h * HEAD_DIM, (h + 1) * HEAD_DIM)
        s = _dot_nt(q_ref[:, sl], k_ref[:, sl]) + bias
        m_prev = m_ref[h]
        m_new = jnp.maximum(m_prev, jnp.max(s, axis=1, keepdims=True))
        alpha = jnp.exp(m_prev - m_new)
        p = jnp.exp(s - m_new)
        l_ref[h] = alpha * l_ref[h] + jnp.sum(p, axis=1, keepdims=True)
        acc_ref[:, sl] = alpha * acc_ref[:, sl] + _dot(p.astype(MXU_DTYPE), v_ref[:, sl])
        m_ref[h] = m_new

    @pl.when(kt == last_kt)
    def _():
        for h in range(A_HEADS):
            sl = slice(h * HEAD_DIM, (h + 1) * HEAD_DIM)
            o_ref[:, sl] = (acc_ref[:, sl] / l_ref[h]).astype(o_ref.dtype)


def _mix_kernel(x_ref, a_ref, pu_ref, halo_ref, qm_ref, kvm_ref, pw_ref, ps_ref, wo_ref, fg_ref,
                h_ref, hn_ref, *, tm, seq):
    i = pl.program_id(0)
    tiles_per_seq = seq // tm
    pos0 = (i % tiles_per_seq) * tm
    halo_rows = halo_ref.shape[0]
    a_w = a_ref.shape[1]
    p_w = pu_ref.shape[1]
    grp = p_w // len(POOL_WINDOWS)

    halo = jnp.where(pos0 > 0, halo_ref[...], 0.0)
    ext = jnp.concatenate([halo, pu_ref[...]], axis=0)
    pos = pos0 + lax.broadcasted_iota(I32, (tm, grp), 0)
    out = x_ref[...] + _dot(a_ref[...], wo_ref[0:a_w, :])
    run = ext
    width = 1
    for g, w in enumerate(POOL_WINDOWS):
        while width < w:
            run = run + pltpu.roll(run, width, 0)
            width *= 2
        sl = slice(g * grp, (g + 1) * grp)
        wsum = run[halo_rows:, sl]
        cnt = jnp.minimum(pos + 1, w).astype(F32)
        d = wsum / cnt - pu_ref[:, sl]
        pg = _dot(d.astype(MXU_DTYPE), pw_ref[g]) * ps_ref[:, sl]
        out = out + _dot(pg.astype(MXU_DTYPE), wo_ref[a_w + g * grp:a_w + (g + 1) * grp, :])

    m_w = qm_ref.shape[1]
    for h in range(M_HEADS):
        sl = slice(h * HEAD_DIM, (h + 1) * HEAD_DIM)
        s = _dot_nt(qm_ref[:, sl], kvm_ref[:, sl])
        p = jnp.exp(s - jnp.max(s, axis=1, keepdims=True))
        o = _dot(p.astype(MXU_DTYPE), kvm_ref[:, m_w + h * HEAD_DIM:m_w + (h + 1) * HEAD_DIM])
        o = o / jnp.sum(p, axis=1, keepdims=True)
        r0 = a_w + p_w + h * HEAD_DIM
        out = out + _dot(o.astype(MXU_DTYPE), wo_ref[r0:r0 + HEAD_DIM, :])

    h_ref[...] = out
    hn_ref[...] = _rms(out, fg_ref[...]).astype(hn_ref.dtype)


def _top16_rows(s, row):
    rank = jnp.full(s.shape, float(PEER_TOPK), F32)
    big = jnp.int32(s.shape[0])
    vals = []
    for k in range(PEER_TOPK):
        m = jnp.max(s, axis=0, keepdims=True)
        first = jnp.min(jnp.where(s == m, row, big), axis=0, keepdims=True)
        hit = row == first
        rank = jnp.where(hit, float(k), rank)
        s = jnp.where(hit, -jnp.inf, s)
        vals.append(m)
    return jnp.concatenate(vals, axis=0), rank


def _route_kernel(hn_ref, wq_ref, k1_ref, k2_ref, r2_ref, e2_ref, n_ref, c_ref, *, tm):
    q = _dot(hn_ref[...], wq_ref[...]).astype(MXU_DTYPE)
    row = lax.broadcasted_iota(I32, (N_KEYS, tm), 0)
    sub = lax.broadcasted_iota(I32, (8, tm), 0)
    half = N_KEYS
    for h in range(PEER_HEADS):
        s1 = _dot_nt(k1_ref[...], q[:, h * 2 * half:h * 2 * half + half])
        s2 = _dot_nt(k2_ref[...], q[:, h * 2 * half + half:(h + 1) * 2 * half])
        v1, rank1 = _top16_rows(s1, row)
        v2, rank2 = _top16_rows(s2, row)

        blocks, flats = [], []
        for half_a in range(2):
            blocks.append(v1[half_a * 8:(half_a + 1) * 8] + v2[0:1])
            flats.append((sub + half_a * 8) * PEER_TOPK)
        for b in range(1, 8):
            blk = v1[0:8] + v2[b:b + 1]
            blocks.append(jnp.where(sub < PEER_TOPK // (b + 1), blk, -jnp.inf))
            flats.append(sub * PEER_TOPK + b)
        blocks.append(v1[0:1] + v2[8:16])
        flats.append(sub + 8)
        cand = jnp.concatenate(blocks, axis=0)
        flat = jnp.concatenate(flats, axis=0)
        cand0 = cand
        sel = jnp.zeros(cand.shape, jnp.bool_)
        for _ in range(PEER_TOPK):
            m = jnp.max(cand, axis=0, keepdims=True)
            first = jnp.min(jnp.where(cand == m, flat, PEER_TOPK * PEER_TOPK), axis=0, keepdims=True)
            hit = flat == first
            sel = jnp.logical_or(sel, hit)
            cand = jnp.where(hit, -jnp.inf, cand)
        top = v1[0:1] + v2[0:1]
        z = jnp.sum(jnp.where(sel, jnp.exp(cand0 - top), 0.0), axis=0, keepdims=True)
        self32 = sel.astype(F32)
        nb_lo = self32[0:8]
        for blk in range(2, 9):
            nb_lo = nb_lo + self32[blk * 8:(blk + 1) * 8]
        tail = jnp.sum(self32[72:80], axis=0, keepdims=True)
        nb_lo = nb_lo + jnp.where(sub == 0, tail, 0.0)
        nb = jnp.concatenate([nb_lo, self32[8:16]], axis=0)

        n_dense = jnp.zeros((N_KEYS, tm), F32)
        for a in range(PEER_TOPK):
            n_dense = jnp.where(rank1 == float(a), nb[a:a + 1], n_dense)
        r2_ref[h] = rank2
        e2_ref[h] = jnp.exp(s2 - v2[0:1])
        n_ref[h] = n_dense
        c_ref[h] = jnp.exp(s1 - v1[0:1]) / z


def _gelu(x):
    return 0.5 * x * (1.0 + lax.erf(x * (2.0 ** -0.5)))


def _peer_kernel(hn_ref, u_ref, vt_ref, r2_ref, e2_ref, n_ref, c_ref, h_ref, o_ref, acc_ref, at_ref, *, tb, eb):
    ei = pl.program_id(1)

    @pl.when(ei == 0)
    def _():
        acc_ref[...] = jnp.zeros(acc_ref.shape, F32)

    ht = _dot_nt(u_ref[...], hn_ref[...])
    ipb = eb // N_KEYS
    sub = 8
    base = pl.multiple_of((ei * ipb) // sub * sub, sub)
    off = (ei * ipb) % sub
    for il in range(ipb):
        for tcol in range(tb // LANES):
            cs = slice(tcol * LANES, (tcol + 1) * LANES)
            gate = jnp.zeros((N_KEYS, LANES), F32)
            for h in range(PEER_HEADS):
                n8 = n_ref[h, pl.ds(base, sub), cs]
                c8 = c_ref[h, pl.ds(base, sub), cs]
                nrow = n8[il:il + 1]
                crow = c8[il:il + 1]
                for o in range(ipb, sub, ipb):
                    nrow = jnp.where(off == o, n8[o + il:o + il + 1], nrow)
                    crow = jnp.where(off == o, c8[o + il:o + il + 1], crow)
                gate = gate + jnp.where(r2_ref[h, :, cs] < nrow, e2_ref[h, :, cs], 0.0) * crow
            a = gate * _gelu(ht[il * N_KEYS:(il + 1) * N_KEYS, cs])
            at_ref[il * N_KEYS:(il + 1) * N_KEYS, cs] = a.astype(at_ref.dtype)
    acc_ref[...] += _dot(vt_ref[...], at_ref[...])

    @pl.when(ei == pl.num_programs(1) - 1)
    def _():
        o_ref[...] = h_ref[...] + acc_ref[...].T


def _rope_tables(seq):
    pos = jnp.arange(seq, dtype=F32)[:, None]

    def table(half):
        inv = ROPE_THETA ** (-jnp.arange(half, dtype=F32) / half)
        ang = pos * inv[None, :]
        return jnp.cos(ang), jnp.sin(ang)

    c64, s64 = table(HEAD_DIM // 2)
    c32, s32 = table(IDX_DIM // 2)
    cos_a = jnp.concatenate([c64, c64], axis=1)
    sin_a = jnp.concatenate([-s64, s64], axis=1)
    cos_i = jnp.concatenate([c32, c32, c32, c32], axis=1)
    sin_i = jnp.concatenate([-s32, s32, -s32, s32], axis=1)
    return cos_a, sin_a, cos_i, sin_i


def _attn_steps(seq, tq, tk):
    qts, kts = [], []
    for qt in range(seq // tq):
        for kt in range(((qt + 1) * tq - 1) // tk + 1):
            qts.append(qt)
            kts.append(kt)
    return np.asarray(qts, np.int32), np.asarray(kts, np.int32)


def _full(shape):
    return pl.BlockSpec(shape, lambda *_: (0,) * len(shape))


def _layer(x, mem, attn_g, w_in, q_g, k_g, pool_w, pool_s, mem_g, w_mem_kv, mq_g, mk_g, w_out,
           ffn_g, peer_wq, sk1, sk2, peer_u, peer_v):
    b, seq, d = x.shape
    t = b * seq
    mem_len = mem.shape[1]
    a_w = A_HEADS * HEAD_DIM
    i_w = IDX_HEADS * IDX_DIM
    p_w = pool_w.shape[0] * pool_w.shape[1]
    m_w = M_HEADS * HEAD_DIM
    top_k = min(TOP_K_MAX, seq // 4)
    md = MXU_DTYPE

    x2 = x.reshape(t, d)
    row = lambda v: v.reshape(1, -1).astype(F32)
    cos_a, sin_a, cos_i, sin_i = _rope_tables(seq)

    o = 0
    w_qkv = w_in[:, o:o + 3 * a_w].astype(md); o += 3 * a_w
    w_qi = w_in[:, o:o + i_w].astype(md); o += i_w
    w_small = jnp.pad(w_in[:, o:o + IDX_DIM + IDX_HEADS], ((0, 0), (0, LANES - IDX_DIM - IDX_HEADS))).astype(md)
    o += IDX_DIM + IDX_HEADS
    w_pu = w_in[:, o:o + p_w].astype(md); o += p_w
    w_qm = w_in[:, o:o + m_w].astype(md)

    tm = min(1024, seq)
    n_rt = t // tm
    pos_blk = lambda i: i % (seq // tm)
    hg = jnp.stack([row(q_g) * (HEAD_DIM ** -0.5), row(k_g), jnp.ones((1, HEAD_DIM), F32)])
    qkv = pl.pallas_call(
        _qkv_kernel,
        grid=(n_rt, 3),
        in_specs=[pl.BlockSpec((tm, d), lambda i, j: (i, 0)),
                  pl.BlockSpec((1, d), lambda i, j: (0, 0)),
                  pl.BlockSpec((d, a_w), lambda i, j: (0, j)),
                  pl.BlockSpec((1, 1, HEAD_DIM), lambda i, j: (j, 0, 0)),
                  pl.BlockSpec((tm, HEAD_DIM), lambda i, j: (pos_blk(i), 0)),
                  pl.BlockSpec((tm, HEAD_DIM), lambda i, j: (pos_blk(i), 0))],
        out_specs=pl.BlockSpec((tm, a_w), lambda i, j: (i, j)),
        out_shape=jax.ShapeDtypeStruct((t, 3 * a_w), md),
        scratch_shapes=[pltpu.VMEM((tm, d), md)],
        compiler_params=_cparams("parallel", "arbitrary"),
        name="qkv_proj",
    )(x2, row(attn_g), w_qkv, hg, cos_a, sin_a)

    expand = np.zeros((LANES, i_w), np.float32)
    for h in range(IDX_HEADS):
        expand[IDX_DIM + h, h * IDX_DIM:(h + 1) * IDX_DIM] = 1.0
    tmi = min(512, seq)
    qi, kd, wv = pl.pallas_call(
        _idx_kernel,
        grid=(t // tmi,),
        in_specs=[pl.BlockSpec((tmi, d), lambda i: (i, 0)),
                  _full((1, d)), _full((d, i_w)), _full((d, LANES)), _full((LANES, i_w)),
                  pl.BlockSpec((tmi, LANES), lambda i: (i % (seq // tmi), 0)),
                  pl.BlockSpec((tmi, LANES), lambda i: (i % (seq // tmi), 0))],
        out_specs=[pl.BlockSpec((tmi, i_w), lambda i: (i, 0)),
                   pl.BlockSpec((tmi, LANES), lambda i: (i, 0)),
                   pl.BlockSpec((tmi, LANES), lambda i: (i, 0))],
        out_shape=[jax.ShapeDtypeStruct((t, i_w), md),
                   jax.ShapeDtypeStruct((t, LANES), md),
                   jax.ShapeDtypeStruct((t, LANES), F32)],
        compiler_params=_cparams("parallel"),
        name="idx_proj",
    )(x2, row(attn_g), w_qi, w_small, jnp.asarray(expand, md), cos_i, sin_i)

    pu = pl.pallas_call(
        _plain_kernel,
        grid=(t // tmi,),
        in_specs=[pl.BlockSpec((tmi, d), lambda i: (i, 0)), _full((1, d)), _full((d, p_w))],
        out_specs=pl.BlockSpec((tmi, p_w), lambda i: (i, 0)),
        out_shape=jax.ShapeDtypeStruct((t, p_w), F32),
        compiler_params=_cparams("parallel"),
        name="pool_proj",
    )(x2, row(attn_g), w_pu)

    qm = pl.pallas_call(
        functools.partial(_headnorm_kernel, heads=M_HEADS),
        grid=(t // tmi,),
        in_specs=[pl.BlockSpec((tmi, d), lambda i: (i, 0)), _full((1, d)), _full((d, m_w)), _full((1, HEAD_DIM))],
        out_specs=pl.BlockSpec((tmi, m_w), lambda i: (i, 0)),
        out_shape=jax.ShapeDtypeStruct((t, m_w), md),
        compiler_params=_cparams("parallel"),
        name="memq_proj",
    )(x2, row(attn_g), w_qm, row(mq_g) * (HEAD_DIM ** -0.5))

    tmem = b * mem_len
    kvm = pl.pallas_call(
        _memkv_kernel,
        grid=(2,),
        in_specs=[_full((tmem, d)), _full((1, d)),
                  pl.BlockSpec((d, m_w), lambda j: (0, j)), _full((1, HEAD_DIM))],
        out_specs=pl.BlockSpec((tmem, m_w), lambda j: (0, j)),
        out_shape=jax.ShapeDtypeStruct((tmem, 2 * m_w), md),
        compiler_params=_cparams("arbitrary"),
        name="memkv_proj",
    )(mem.reshape(tmem, d), row(mem_g), w_mem_kv.astype(md), row(mk_g))

    tq, tks = 256, 512
    mask = pl.pallas_call(
        functools.partial(_select_kernel, tq=tq, tk=tks, seq=seq, top_k=top_k),
        grid=(b, seq // tq),
        in_specs=[pl.BlockSpec((None, tq, i_w), lambda bi, qt: (bi, qt, 0)),
                  pl.BlockSpec((None, tq, LANES), lambda bi, qt: (bi, qt, 0)),
                  pl.BlockSpec((None, seq, LANES), lambda bi, qt: (bi, 0, 0))],
        out_specs=pl.BlockSpec((None, tq, seq), lambda bi, qt: (bi, qt, 0)),
        out_shape=jax.ShapeDtypeStruct((b, seq, seq), jnp.int8),
        scratch_shapes=[pltpu.VMEM((IDX_HEADS * tq, LANES), md),
                        pltpu.VMEM((IDX_HEADS, tq, LANES), F32),
                        pltpu.VMEM((IDX_HEADS, tq, LANES), F32),
                        pltpu.VMEM((tq, seq), jnp.int16),
                        pltpu.VMEM((tq, seq), jnp.int16)],
        compiler_params=_cparams("parallel", "arbitrary"),
        name="dsa_select",
    )(qi.reshape(b, seq, i_w), wv.reshape(b, seq, LANES), kd.reshape(b, seq, LANES))

    tka = 512
    qts, kts = _attn_steps(seq, tq, tka)
    qkv3 = qkv.reshape(b, seq, 3 * a_w)
    a_out = pl.pallas_call(
        functools.partial(_attn_kernel, tq=tq, tk=tka),
        grid_spec=pltpu.PrefetchScalarGridSpec(
            num_scalar_prefetch=2,
            grid=(b, len(qts)),
            in_specs=[pl.BlockSpec((None, tq, a_w), lambda bi, s, qt, kt: (bi, qt[s], 0)),
                      pl.BlockSpec((None, tka, a_w), lambda bi, s, qt, kt: (bi, kt[s], 1)),
                      pl.BlockSpec((None, tka, a_w), lambda bi, s, qt, kt: (bi, kt[s], 2)),
                      pl.BlockSpec((None, tq, tka), lambda bi, s, qt, kt: (bi, qt[s], kt[s]))],
            out_specs=pl.BlockSpec((None, tq, a_w), lambda bi, s, qt, kt: (bi, qt[s], 0)),
            scratch_shapes=[pltpu.VMEM((A_HEADS, tq, 1), F32),
                            pltpu.VMEM((A_HEADS, tq, 1), F32),
                            pltpu.VMEM((tq, a_w), F32)]),
        out_shape=jax.ShapeDtypeStruct((b, seq, a_w), md),
        compiler_params=_cparams("parallel", "arbitrary"),
        name="dsa_attn",
    )(jnp.asarray(qts), jnp.asarray(kts), qkv3, qkv3, qkv3, mask)

    tmx = min(512, seq)
    halo_rows = POOL_WINDOWS[-1]
    hsub = tmx // halo_rows
    h_res, hn = pl.pallas_call(
        functools.partial(_mix_kernel, tm=tmx, seq=seq),
        grid=(t // tmx,),
        in_specs=[pl.BlockSpec((tmx, d), lambda i: (i, 0)),
                  pl.BlockSpec((tmx, a_w), lambda i: (i, 0)),
                  pl.BlockSpec((tmx, p_w), lambda i: (i, 0)),
                  pl.BlockSpec((halo_rows, p_w), lambda i: (jnp.maximum(i * hsub - 1, 0), 0)),
                  pl.BlockSpec((tmx, m_w), lambda i: (i, 0)),
                  pl.BlockSpec((mem_len, 2 * m_w), lambda i: (i // (seq // tmx), 0)),
                  _full(pool_w.shape), _full((1, p_w)), _full((d, d)), _full((1, d))],
        out_specs=[pl.BlockSpec((tmx, d), lambda i: (i, 0)),
                   pl.BlockSpec((tmx, d), lambda i: (i, 0))],
        out_shape=[jax.ShapeDtypeStruct((t, d), F32), jax.ShapeDtypeStruct((t, d), md)],
        compiler_params=_cparams("parallel"),
        name="mix_out",
    )(x2, a_out.reshape(t, a_w), pu, pu, qm, kvm, pool_w.astype(md), row(pool_s), w_out.astype(md), row(ffn_g))

    tmr = 256
    rshape = jax.ShapeDtypeStruct((PEER_HEADS, N_KEYS, t), F32)
    rspec = pl.BlockSpec((PEER_HEADS, N_KEYS, tmr), lambda i: (0, 0, i))
    r2, e2, nsel, csel = pl.pallas_call(
        functools.partial(_route_kernel, tm=tmr),
        grid=(t // tmr,),
        in_specs=[pl.BlockSpec((tmr, d), lambda i: (i, 0)),
                  _full(peer_wq.shape), _full(sk1.shape), _full(sk2.shape)],
        out_specs=[rspec, rspec, rspec, rspec],
        out_shape=[rshape, rshape, rshape, rshape],
        compiler_params=_cparams("parallel"),
        name="peer_route",
    )(hn, peer_wq.astype(md), sk1.astype(md), sk2.astype(md))

    tb, eb = 512, 512
    n_exp = peer_u.shape[0]
    gspec = pl.BlockSpec((PEER_HEADS, N_KEYS, tb), lambda ti, ei: (0, 0, ti))
    out = pl.pallas_call(
        functools.partial(_peer_kernel, tb=tb, eb=eb),
        grid=(t // tb, n_exp // eb),
        in_specs=[pl.BlockSpec((tb, d), lambda ti, ei: (ti, 0)),
                  pl.BlockSpec((eb, d), lambda ti, ei: (ei, 0)),
                  pl.BlockSpec((d, eb), lambda ti, ei: (0, ei)),
                  gspec, gspec, gspec, gspec,
                  pl.BlockSpec((tb, d), lambda ti, ei: (ti, 0))],
        out_specs=pl.BlockSpec((tb, d), lambda ti, ei: (ti, 0)),
        out_shape=jax.ShapeDtypeStruct((t, d), F32),
        scratch_shapes=[pltpu.VMEM((d, tb), F32), pltpu.VMEM((eb, tb), md)],
        compiler_params=_cparams("parallel", "arbitrary"),
        name="peer_experts",
    )(hn, peer_u.astype(md), peer_v.T.astype(md), r2, e2, nsel, csel, h_res)
    return out.reshape(b, seq, d)


def kernel(x, mem, attn_norm_gain, w_in, q_norm_gain, k_norm_gain, pool_w, pool_scale, mem_norm_gain,
           w_mem_kv, mq_norm_gain, mk_norm_gain, w_out, ffn_norm_gain, peer_w_q, peer_sub_keys_1,
           peer_sub_keys_2, peer_u, peer_v):
    for l in range(attn_norm_gain.shape[0]):
        x = _layer(x, mem, attn_norm_gain[l], w_in[l], q_norm_gain[l], k_norm_gain[l], pool_w[l],
                   pool_scale[l], mem_norm_gain[l], w_mem_kv[l], mq_norm_gain[l], mk_norm_gain[l],
                   w_out[l], ffn_norm_gain[l], peer_w_q[l], peer_sub_keys_1[l], peer_sub_keys_2[l],
                   peer_u[l], peer_v[l])
    return x
```

```python
import functools

import numpy as np
import jax
import jax.numpy as jnp
from jax import lax
from jax.experimental import pallas as pl
from jax.experimental.pallas import tpu as pltpu

F32 = jnp.float32
I32 = jnp.int32
MXU_DTYPE = jnp.bfloat16

LANES = 128
VMEM_LIMIT = 56 * 1024 * 1024

EPS = 1e-6
ROPE_THETA = 10000.0
CHUNK = 64
A_HEADS = 8
HEAD_DIM = 128
IDX_HEADS = 16
IDX_DIM = 64
TOP_K_MAX = 256
POOL_WINDOWS = (2, 4, 8, 16)
M_HEADS = 4
PEER_HEADS = 8
N_KEYS = 128
PEER_TOPK = 16
INT_MIN = -(2 ** 31)
NEG_BIG = -1e30


def _cparams(*sem):
    return pltpu.CompilerParams(dimension_semantics=sem, vmem_limit_bytes=VMEM_LIMIT)


def _rms(x, g):
    ms = jnp.mean(x * x, axis=-1, keepdims=True)
    return x * lax.rsqrt(ms + EPS) * g


def _dot(a, b):
    return jnp.dot(a, b, preferred_element_type=F32)


def _dot_nt(a, b):
    return lax.dot_general(a, b, (((1,), (1,)), ((), ())), preferred_element_type=F32)


def _lane_iota(shape):
    return lax.broadcasted_iota(I32, shape, len(shape) - 1)


def _qk_kernel(x_ref, g_ref, w_ref, hg_ref, cos_ref, sin_ref, o_ref, xn_ref):
    j = pl.program_id(1)

    @pl.when(j == 0)
    def _():
        xn_ref[...] = _rms(x_ref[...], g_ref[...]).astype(xn_ref.dtype)

    z = _dot(xn_ref[...], w_ref[...])

    cosf = cos_ref[...]
    sinf = sin_ref[...]
    hg = hg_ref[0]
    for h in range(A_HEADS):
        sl = slice(h * HEAD_DIM, (h + 1) * HEAD_DIM)
        n = _rms(z[:, sl], hg)
        o_ref[:, sl] = (n * cosf + pltpu.roll(n, HEAD_DIM // 2, 1) * sinf).astype(o_ref.dtype)


def _vt_kernel(x_ref, g_ref, wt_ref, o_ref):
    xn = _rms(x_ref[...], g_ref[...]).astype(MXU_DTYPE)
    o_ref[...] = _dot_nt(wt_ref[...], xn).astype(o_ref.dtype)


def _rope64(xb, cos4, sin4, first_half):
    rot = jnp.where(first_half, pltpu.roll(xb, LANES - IDX_DIM // 2, 1), pltpu.roll(xb, IDX_DIM // 2, 1))
    return xb * cos4 + rot * sin4


def _split3(v):
    hi = v.astype(MXU_DTYPE)
    r1 = v - hi.astype(F32)
    mid = r1.astype(MXU_DTYPE)
    lo = (r1 - mid.astype(F32)).astype(MXU_DTYPE)
    return hi, mid, lo


def _idx_kernel(x_ref, g_ref, wq_ref, ws_ref, e_ref, cos_ref, sin_ref, qi_ref, kd_ref, wt_ref):
    xn = _rms(x_ref[...], g_ref[...]).astype(MXU_DTYPE)
    zq = _dot(xn, wq_ref[...])
    zs = _dot(xn, ws_ref[...])
    wt_ref[...] = zs.T
    hi, mid, lo = _split3(zs)
    e = e_ref[...]
    wexp = _dot(hi, e) + _dot(mid, e) + _dot(lo, e)
    cos4 = cos_ref[...]
    sin4 = sin_ref[...]
    lane = _lane_iota(zs.shape)
    first_half = (lane & (IDX_DIM // 2)) == 0
    scale = (IDX_DIM ** -0.5) * (IDX_HEADS ** -0.5)
    for c in range(zq.shape[1] // LANES):
        sl = slice(c * LANES, (c + 1) * LANES)
        y = _rope64(zq[:, sl], cos4, sin4, first_half)
        qi_ref[:, sl] = (y * (wexp[:, sl] * scale)).astype(qi_ref.dtype)
    yk = _rope64(zs, cos4, sin4, first_half)
    kd_ref[...] = jnp.where(lane < IDX_DIM, yk, pltpu.roll(yk, IDX_DIM, 1)).astype(kd_ref.dtype)


def _plain_kernel(x_ref, g_ref, w_ref, o_ref):
    xn = _rms(x_ref[...], g_ref[...]).astype(MXU_DTYPE)
    o_ref[...] = _dot(xn, w_ref[...]).astype(o_ref.dtype)


def _headnorm_kernel(x_ref, g_ref, w_ref, hg_ref, o_ref, *, heads):
    xn = _rms(x_ref[...], g_ref[...]).astype(MXU_DTYPE)
    z = _dot(xn, w_ref[...])
    hg = hg_ref[...]
    for h in range(heads):
        sl = slice(h * HEAD_DIM, (h + 1) * HEAD_DIM)
        o_ref[:, sl] = _rms(z[:, sl], hg).astype(o_ref.dtype)


def _memkv_kernel(x_ref, g_ref, w_ref, hg_ref, o_ref):
    j = pl.program_id(0)
    xn = _rms(x_ref[...], g_ref[...]).astype(MXU_DTYPE)
    z = _dot(xn, w_ref[...])

    @pl.when(j == 0)
    def _():
        hg = hg_ref[...]
        for h in range(M_HEADS):
            sl = slice(h * HEAD_DIM, (h + 1) * HEAD_DIM)
            o_ref[:, sl] = _rms(z[:, sl], hg).astype(o_ref.dtype)

    @pl.when(j == 1)
    def _():
        o_ref[...] = z.astype(o_ref.dtype)


def _select_kernel(qi_ref, wt_ref, kd_ref, mask_ref, lhs_ref, lo_ref, hi_ref, key_ref, *, tq, tk, seq, top_k):
    qt = pl.program_id(1)
    t0 = qt * tq
    n_kt = (t0 + tq + tk - 1) // tk
    hgrp = 4
    rb = 128
    n_rb = tk // rb
    cr = 32

    lane = _lane_iota((tq, LANES))
    for h in range(IDX_HEADS):
        pair = qi_ref[:, (h // 2) * LANES:(h // 2 + 1) * LANES]
        own = (lane < IDX_DIM) if h % 2 == 0 else (lane >= IDX_DIM)
        lhs_ref[h * tq:(h + 1) * tq, :] = jnp.where(own, pair, jnp.zeros_like(pair))
    w = wt_ref[IDX_DIM:IDX_DIM + IDX_HEADS, :]
    lo_ref[...] = jnp.where(w > 0, 0.0, -jnp.inf).astype(F32)
    hi_ref[...] = jnp.where(w > 0, jnp.inf, 0.0).astype(F32)

    q_lim = ((t0 + _lane_iota((1, tq))) // CHUNK + 1) * CHUNK

    def rows(kt, r):
        return pl.ds(pl.multiple_of(kt * tk + r * rb, rb), rb)

    def score_tile(kt, carry):
        k0 = pl.multiple_of(kt * tk, tk)
        kd = kd_ref[pl.ds(k0, tk), :]
        accs = [jnp.zeros((rb, tq), F32) for _ in range(n_rb)]
        for g in range(IDX_HEADS // hgrp):
            raw = _dot_nt(kd, lhs_ref[g * hgrp * tq:(g + 1) * hgrp * tq, :])
            for hh in range(hgrp):
                h = g * hgrp + hh
                lo = lo_ref[h:h + 1, :]
                hi = hi_ref[h:h + 1, :]
                for r in range(n_rb):
                    blk = raw[r * rb:(r + 1) * rb, hh * tq:(hh + 1) * tq]
                    accs[r] = accs[r] + jnp.minimum(jnp.maximum(blk, lo), hi)
        for r in range(n_rb):
            bits = pltpu.bitcast(accs[r], I32)
            key = bits ^ ((bits >> 31) & 0x7FFFFFFF)
            srow = k0 + r * rb + lax.broadcasted_iota(I32, (rb, tq), 0)
            key_ref[rows(kt, r), :] = jnp.where(srow < q_lim, key, INT_MIN)
        return carry

    lax.fori_loop(0, n_kt, score_tile, 0)

    def bit_step(i, prefix):
        cand_u = prefix | lax.shift_left(jnp.int32(1), jnp.int32(31) - i)
        cand = cand_u ^ INT_MIN

        def count_tile(kt, cnt):
            for r in range(n_rb):
                ind = jnp.where(key_ref[rows(kt, r), :] >= cand, 1, 0)
                for c in range(rb // cr):
                    cnt = cnt + ind[c * cr:(c + 1) * cr]
            return cnt

        cnt = lax.fori_loop(0, n_kt, count_tile, jnp.zeros((cr, tq), I32))
        total = jnp.sum(cnt, axis=0, keepdims=True)
        return jnp.where(total >= top_k, cand_u, prefix)

    prefix = lax.fori_loop(0, 32, bit_step, jnp.zeros((1, tq), I32))
    tau = prefix ^ INT_MIN

    def write_tile(kt, carry):
        for r in range(n_rb):
            key = key_ref[rows(kt, r), :]
            sel = jnp.logical_and(key >= tau, key > INT_MIN)
            mask_ref[rows(kt, r), :] = sel.astype(mask_ref.dtype)
        return carry

    lax.fori_loop(0, n_kt, write_tile, 0)

    def zero_tile(kt, carry):
        mask_ref[pl.ds(pl.multiple_of(kt * tk, tk), tk), :] = jnp.zeros((tk, tq), mask_ref.dtype)
        return carry

    lax.fori_loop(n_kt, seq // tk, zero_tile, 0)


def _attn_kernel(qt_tab, kt_tab, q_ref, k_ref, vt_ref, mask_ref, o_ref, m_ref, l_ref, acc_ref, s_ref, *, tq, tk):
    step = pl.program_id(1)
    qt = qt_tab[step]
    kt = kt_tab[step]
    last_kt = ((qt + 1) * tq - 1) // tk

    @pl.when(kt == 0)
    def _():
        m_ref[...] = jnp.full(m_ref.shape, NEG_BIG, F32)
        l_ref[...] = jnp.zeros(l_ref.shape, F32)
        acc_ref[...] = jnp.zeros(acc_ref.shape, F32)

    bias = jnp.where(mask_ref[...].astype(I32) != 0, 0.0, -jnp.inf).astype(F32)
    hs = lambda h: slice(h * HEAD_DIM, (h + 1) * HEAD_DIM)

    def logits(h):
        s_ref[h % 2] = _dot_nt(k_ref[:, hs(h)], q_ref[:, hs(h)])

    logits(0)
    for h in range(A_HEADS):
        if h + 1 < A_HEADS:
            logits(h + 1)
        s = s_ref[h % 2] + bias
        m_prev = m_ref[h:h + 1, :]
        m_new = jnp.maximum(m_prev, jnp.max(s, axis=0, keepdims=True))
        alpha = jnp.exp2(m_prev - m_new)
        p = jnp.exp2(s - m_new)
        l_ref[h:h + 1, :] = alpha * l_ref[h:h + 1, :] + jnp.sum(p, axis=0, keepdims=True)
        acc_ref[hs(h), :] = alpha * acc_ref[hs(h), :] + _dot(vt_ref[hs(h), :], p.astype(MXU_DTYPE))
        m_ref[h:h + 1, :] = m_new

    @pl.when(kt == last_kt)
    def _():
        for h in range(A_HEADS):
            o_ref[:, hs(h)] = (acc_ref[hs(h), :] / l_ref[h:h + 1, :]).T.astype(o_ref.dtype)


def _mix_kernel(x_ref, a_ref, pu_ref, halo_ref, qm_ref, kvm_ref, pw_ref, ps_ref, wo_ref, fg_ref,
                h_ref, hn_ref, *, tm, seq):
    i = pl.program_id(0)
    tiles_per_seq = seq // tm
    pos0 = (i % tiles_per_seq) * tm
    halo_rows = halo_ref.shape[0]
    a_w = a_ref.shape[1]
    p_w = pu_ref.shape[1]
    grp = p_w // len(POOL_WINDOWS)

    halo = jnp.where(pos0 > 0, halo_ref[...], 0.0)
    ext = jnp.concatenate([halo, pu_ref[...]], axis=0)
    pos = pos0 + lax.broadcasted_iota(I32, (tm, grp), 0)
    out = x_ref[...] + _dot(a_ref[...], wo_ref[0:a_w, :])
    run = ext
    width = 1
    for g, w in enumerate(POOL_WINDOWS):
        while width < w:
            run = run + pltpu.roll(run, width, 0)
            width *= 2
        sl = slice(g * grp, (g + 1) * grp)
        wsum = run[halo_rows:, sl]
        cnt = jnp.minimum(pos + 1, w).astype(F32)
        d = wsum / cnt - pu_ref[:, sl]
        pg = _dot(d.astype(MXU_DTYPE), pw_ref[g]) * ps_ref[:, sl]
        out = out + _dot(pg.astype(MXU_DTYPE), wo_ref[a_w + g * grp:a_w + (g + 1) * grp, :])

    m_w = qm_ref.shape[1]
    for h in range(M_HEADS):
        sl = slice(h * HEAD_DIM, (h + 1) * HEAD_DIM)
        s = _dot_nt(qm_ref[:, sl], kvm_ref[:, sl])
        p = jnp.exp(s - jnp.max(s, axis=1, keepdims=True))
        o = _dot(p.astype(MXU_DTYPE), kvm_ref[:, m_w + h * HEAD_DIM:m_w + (h + 1) * HEAD_DIM])
        o = o / jnp.sum(p, axis=1, keepdims=True)
        r0 = a_w + p_w + h * HEAD_DIM
        out = out + _dot(o.astype(MXU_DTYPE), wo_ref[r0:r0 + HEAD_DIM, :])

    h_ref[...] = out
    hn_ref[...] = _rms(out, fg_ref[...]).astype(hn_ref.dtype)


def _top16_rows(s, row):
    rank = jnp.full(s.shape, float(PEER_TOPK), F32)
    big = jnp.int32(s.shape[0])
    vals = []
    for k in range(PEER_TOPK):
        m = jnp.max(s, axis=0, keepdims=True)
        first = jnp.min(jnp.where(s == m, row, big), axis=0, keepdims=True)
        hit = row == first
        rank = jnp.where(hit, float(k), rank)
        s = jnp.where(hit, -jnp.inf, s)
        vals.append(m)
    return jnp.concatenate(vals, axis=0), rank


def _route_kernel(hn_ref, wq_ref, k1_ref, k2_ref, r2_ref, e2_ref, n_ref, c_ref, *, tm):
    q = _dot(hn_ref[...], wq_ref[...]).astype(MXU_DTYPE)
    row = lax.broadcasted_iota(I32, (N_KEYS, tm), 0)
    sub = lax.broadcasted_iota(I32, (8, tm), 0)
    half = N_KEYS
    for h in range(PEER_HEADS):
        s1 = _dot_nt(k1_ref[...], q[:, h * 2 * half:h * 2 * half + half])
        s2 = _dot_nt(k2_ref[...], q[:, h * 2 * half + half:(h + 1) * 2 * half])
        v1, rank1 = _top16_rows(s1, row)
        v2, rank2 = _top16_rows(s2, row)

        blocks, flats = [], []
        for half_a in range(2):
            blocks.append(v1[half_a * 8:(half_a + 1) * 8] + v2[0:1])
            flats.append((sub + half_a * 8) * PEER_TOPK)
        for b in range(1, 8):
            blk = v1[0:8] + v2[b:b + 1]
            blocks.append(jnp.where(sub < PEER_TOPK // (b + 1), blk, -jnp.inf))
            flats.append(sub * PEER_TOPK + b)
        blocks.append(v1[0:1] + v2[8:16])
        flats.append(sub + 8)
        cand = jnp.concatenate(blocks, axis=0)
        flat = jnp.concatenate(flats, axis=0)
        cand0 = cand
        sel = jnp.zeros(cand.shape, jnp.bool_)
        for _ in range(PEER_TOPK):
            m = jnp.max(cand, axis=0, keepdims=True)
            first = jnp.min(jnp.where(cand == m, flat, PEER_TOPK * PEER_TOPK), axis=0, keepdims=True)
            hit = flat == first
            sel = jnp.logical_or(sel, hit)
            cand = jnp.where(hit, -jnp.inf, cand)
        top = v1[0:1] + v2[0:1]
        z = jnp.sum(jnp.where(sel, jnp.exp(cand0 - top), 0.0), axis=0, keepdims=True)
        self32 = sel.astype(F32)
        nb_lo = self32[0:8]
        for blk in range(2, 9):
            nb_lo = nb_lo + self32[blk * 8:(blk + 1) * 8]
        tail = jnp.sum(self32[72:80], axis=0, keepdims=True)
        nb_lo = nb_lo + jnp.where(sub == 0, tail, 0.0)
        nb = jnp.concatenate([nb_lo, self32[8:16]], axis=0)

        n_dense = jnp.zeros((N_KEYS, tm), F32)
        for a in range(PEER_TOPK):
            n_dense = jnp.where(rank1 == float(a), nb[a:a + 1], n_dense)
        r2_ref[h] = rank2
        e2_ref[h] = jnp.exp(s2 - v2[0:1])
        n_ref[h] = n_dense
        c_ref[h] = jnp.exp(s1 - v1[0:1]) / z


def _gelu(x):
    return 0.5 * x * (1.0 + lax.erf(x * (2.0 ** -0.5)))


def _peer_kernel(hn_ref, u_ref, vt_ref, r2_ref, e2_ref, n_ref, c_ref, h_ref, o_ref, acc_ref, at_ref, *, tb, eb):
    ei = pl.program_id(1)

    @pl.when(ei == 0)
    def _():
        acc_ref[...] = jnp.zeros(acc_ref.shape, F32)

    ht = _dot_nt(u_ref[...], hn_ref[...])
    ipb = eb // N_KEYS
    sub = 8
    base = pl.multiple_of((ei * ipb) // sub * sub, sub)
    off = (ei * ipb) % sub
    for il in range(ipb):
        for tcol in range(tb // LANES):
            cs = slice(tcol * LANES, (tcol + 1) * LANES)
            gate = jnp.zeros((N_KEYS, LANES), F32)
            for h in range(PEER_HEADS):
                n8 = n_ref[h, pl.ds(base, sub), cs]
                c8 = c_ref[h, pl.ds(base, sub), cs]
                nrow = n8[il:il + 1]
                crow = c8[il:il + 1]
                for o in range(ipb, sub, ipb):
                    nrow = jnp.where(off == o, n8[o + il:o + il + 1], nrow)
                    crow = jnp.where(off == o, c8[o + il:o + il + 1], crow)
                gate = gate + jnp.where(r2_ref[h, :, cs] < nrow, e2_ref[h, :, cs], 0.0) * crow
            a = gate * _gelu(ht[il * N_KEYS:(il + 1) * N_KEYS, cs])
            at_ref[il * N_KEYS:(il + 1) * N_KEYS, cs] = a.astype(at_ref.dtype)
    acc_ref[...] += _dot(vt_ref[...], at_ref[...])

    @pl.when(ei == pl.num_programs(1) - 1)
    def _():
        o_ref[...] = h_ref[...] + acc_ref[...].T


def _rope_tables(seq):
    pos = jnp.arange(seq, dtype=F32)[:, None]

    def table(half):
        inv = ROPE_THETA ** (-jnp.arange(half, dtype=F32) / half)
        ang = pos * inv[None, :]
        return jnp.cos(ang), jnp.sin(ang)

    c64, s64 = table(HEAD_DIM // 2)
    c32, s32 = table(IDX_DIM // 2)
    cos_a = jnp.concatenate([c64, c64], axis=1)
    sin_a = jnp.concatenate([-s64, s64], axis=1)
    cos_i = jnp.concatenate([c32, c32, c32, c32], axis=1)
    sin_i = jnp.concatenate([-s32, s32, -s32, s32], axis=1)
    return cos_a, sin_a, cos_i, sin_i


def _attn_steps(seq, tq, tk):
    qts, kts = [], []
    for qt in range(seq // tq):
        for kt in range(((qt + 1) * tq - 1) // tk + 1):
            qts.append(qt)
            kts.append(kt)
    return np.asarray(qts, np.int32), np.asarray(kts, np.int32)


def _full(shape):
    return pl.BlockSpec(shape, lambda *_: (0,) * len(shape))


def _layer(x, mem, attn_g, w_in, q_g, k_g, pool_w, pool_s, mem_g, w_mem_kv, mq_g, mk_g, w_out,
           ffn_g, peer_wq, sk1, sk2, peer_u, peer_v):
    b, seq, d = x.shape
    t = b * seq
    mem_len = mem.shape[1]
    a_w = A_HEADS * HEAD_DIM
    i_w = IDX_HEADS * IDX_DIM
    p_w = pool_w.shape[0] * pool_w.shape[1]
    m_w = M_HEADS * HEAD_DIM
    top_k = min(TOP_K_MAX, seq // 4)
    md = MXU_DTYPE

    x2 = x.reshape(t, d)
    row = lambda v: v.reshape(1, -1).astype(F32)
    cos_a, sin_a, cos_i, sin_i = _rope_tables(seq)

    o = 0
    w_qk = w_in[:, o:o + 2 * a_w].astype(md); o += 2 * a_w
    w_vt = w_in[:, o:o + a_w].T.astype(md); o += a_w
    w_qi = w_in[:, o:o + i_w].astype(md); o += i_w
    w_small = jnp.pad(w_in[:, o:o + IDX_DIM + IDX_HEADS], ((0, 0), (0, LANES - IDX_DIM - IDX_HEADS))).astype(md)
    o += IDX_DIM + IDX_HEADS
    w_pu = w_in[:, o:o + p_w].astype(md); o += p_w
    w_qm = w_in[:, o:o + m_w].astype(md)

    tm = min(1024, seq)
    n_rt = t // tm
    pos_blk = lambda i: i % (seq // tm)
    hg = jnp.stack([row(q_g) * (HEAD_DIM ** -0.5 * np.log2(np.e)), row(k_g)])
    qk = pl.pallas_call(
        _qk_kernel,
        grid=(n_rt, 2),
        in_specs=[pl.BlockSpec((tm, d), lambda i, j: (i, 0)),
                  pl.BlockSpec((1, d), lambda i, j: (0, 0)),
                  pl.BlockSpec((d, a_w), lambda i, j: (0, j)),
                  pl.BlockSpec((1, 1, HEAD_DIM), lambda i, j: (j, 0, 0)),
                  pl.BlockSpec((tm, HEAD_DIM), lambda i, j: (pos_blk(i), 0)),
                  pl.BlockSpec((tm, HEAD_DIM), lambda i, j: (pos_blk(i), 0))],
        out_specs=pl.BlockSpec((tm, a_w), lambda i, j: (i, j)),
        out_shape=jax.ShapeDtypeStruct((t, 2 * a_w), md),
        scratch_shapes=[pltpu.VMEM((tm, d), md)],
        compiler_params=_cparams("parallel", "arbitrary"),
        name="qk_proj",
    )(x2, row(attn_g), w_qk, hg, cos_a, sin_a)

    tmi = min(512, seq)
    vt = pl.pallas_call(
        _vt_kernel,
        grid=(t // tmi,),
        in_specs=[pl.BlockSpec((tmi, d), lambda i: (i, 0)), _full((1, d)), _full((a_w, d))],
        out_specs=pl.BlockSpec((a_w, tmi), lambda i: (0, i)),
        out_shape=jax.ShapeDtypeStruct((a_w, t), md),
        compiler_params=_cparams("parallel"),
        name="vt_proj",
    )(x2, row(attn_g), w_vt)

    expand = np.zeros((LANES, i_w), np.float32)
    for h in range(IDX_HEADS):
        expand[IDX_DIM + h, h * IDX_DIM:(h + 1) * IDX_DIM] = 1.0
    qi, kd, wt = pl.pallas_call(
        _idx_kernel,
        grid=(t // tmi,),
        in_specs=[pl.BlockSpec((tmi, d), lambda i: (i, 0)),
                  _full((1, d)), _full((d, i_w)), _full((d, LANES)), _full((LANES, i_w)),
                  pl.BlockSpec((tmi, LANES), lambda i: (i % (seq // tmi), 0)),
                  pl.BlockSpec((tmi, LANES), lambda i: (i % (seq // tmi), 0))],
        out_specs=[pl.BlockSpec((tmi, i_w), lambda i: (i, 0)),
                   pl.BlockSpec((tmi, LANES), lambda i: (i, 0)),
                   pl.BlockSpec((LANES, tmi), lambda i: (0, i))],
        out_shape=[jax.ShapeDtypeStruct((t, i_w), md),
                   jax.ShapeDtypeStruct((t, LANES), md),
                   jax.ShapeDtypeStruct((LANES, t), F32)],
        compiler_params=_cparams("parallel"),
        name="idx_proj",
    )(x2, row(attn_g), w_qi, w_small, jnp.asarray(expand, md), cos_i, sin_i)

    pu = pl.pallas_call(
        _plain_kernel,
        grid=(t // tmi,),
        in_specs=[pl.BlockSpec((tmi, d), lambda i: (i, 0)), _full((1, d)), _full((d, p_w))],
        out_specs=pl.BlockSpec((tmi, p_w), lambda i: (i, 0)),
        out_shape=jax.ShapeDtypeStruct((t, p_w), F32),
        compiler_params=_cparams("parallel"),
        name="pool_proj",
    )(x2, row(attn_g), w_pu)

    qm = pl.pallas_call(
        functools.partial(_headnorm_kernel, heads=M_HEADS),
        grid=(t // tmi,),
        in_specs=[pl.BlockSpec((tmi, d), lambda i: (i, 0)), _full((1, d)), _full((d, m_w)), _full((1, HEAD_DIM))],
        out_specs=pl.BlockSpec((tmi, m_w), lambda i: (i, 0)),
        out_shape=jax.ShapeDtypeStruct((t, m_w), md),
        compiler_params=_cparams("parallel"),
        name="memq_proj",
    )(x2, row(attn_g), w_qm, row(mq_g) * (HEAD_DIM ** -0.5))

    tmem = b * mem_len
    kvm = pl.pallas_call(
        _memkv_kernel,
        grid=(2,),
        in_specs=[_full((tmem, d)), _full((1, d)),
                  pl.BlockSpec((d, m_w), lambda j: (0, j)), _full((1, HEAD_DIM))],
        out_specs=pl.BlockSpec((tmem, m_w), lambda j: (0, j)),
        out_shape=jax.ShapeDtypeStruct((tmem, 2 * m_w), md),
        compiler_params=_cparams("arbitrary"),
        name="memkv_proj",
    )(mem.reshape(tmem, d), row(mem_g), w_mem_kv.astype(md), row(mk_g))

    tq, tks = 256, 512
    n_qt = seq // tq
    mask = pl.pallas_call(
        functools.partial(_select_kernel, tq=tq, tk=tks, seq=seq, top_k=top_k),
        grid=(b, n_qt),
        in_specs=[pl.BlockSpec((None, tq, i_w), lambda bi, qt: (bi, qt, 0)),
                  pl.BlockSpec((LANES, tq), lambda bi, qt: (0, bi * n_qt + qt)),
                  pl.BlockSpec((None, seq, LANES), lambda bi, qt: (bi, 0, 0))],
        out_specs=pl.BlockSpec((None, seq, tq), lambda bi, qt: (bi, 0, qt)),
        out_shape=jax.ShapeDtypeStruct((b, seq, seq), jnp.int8),
        scratch_shapes=[pltpu.VMEM((IDX_HEADS * tq, LANES), md),
                        pltpu.VMEM((IDX_HEADS, tq), F32),
                        pltpu.VMEM((IDX_HEADS, tq), F32),
                        pltpu.VMEM((seq, tq), I32)],
        compiler_params=_cparams("parallel", "arbitrary"),
        name="dsa_select",
    )(qi.reshape(b, seq, i_w), wt, kd.reshape(b, seq, LANES))

    tka = 512
    n_kta = seq // tka
    qts, kts = _attn_steps(seq, tq, tka)
    qk3 = qk.reshape(b, seq, 2 * a_w)
    a_out = pl.pallas_call(
        functools.partial(_attn_kernel, tq=tq, tk=tka),
        grid_spec=pltpu.PrefetchScalarGridSpec(
            num_scalar_prefetch=2,
            grid=(b, len(qts)),
            in_specs=[pl.BlockSpec((None, tq, a_w), lambda bi, s, qt, kt: (bi, qt[s], 0)),
                      pl.BlockSpec((None, tka, a_w), lambda bi, s, qt, kt: (bi, kt[s], 1)),
                      pl.BlockSpec((a_w, tka), lambda bi, s, qt, kt: (0, bi * n_kta + kt[s])),
                      pl.BlockSpec((None, tka, tq), lambda bi, s, qt, kt: (bi, kt[s], qt[s]))],
            out_specs=pl.BlockSpec((None, tq, a_w), lambda bi, s, qt, kt: (bi, qt[s], 0)),
            scratch_shapes=[pltpu.VMEM((A_HEADS, tq), F32),
                            pltpu.VMEM((A_HEADS, tq), F32),
                            pltpu.VMEM((a_w, tq), F32),
                            pltpu.VMEM((2, tka, tq), F32)]),
        out_shape=jax.ShapeDtypeStruct((b, seq, a_w), md),
        compiler_params=_cparams("parallel", "arbitrary"),
        name="dsa_attn",
    )(jnp.asarray(qts), jnp.asarray(kts), qk3, qk3, vt, mask)

    tmx = min(512, seq)
    halo_rows = POOL_WINDOWS[-1]
    hsub = tmx // halo_rows
    h_res, hn = pl.pallas_call(
        functools.partial(_mix_kernel, tm=tmx, seq=seq),
        grid=(t // tmx,),
        in_specs=[pl.BlockSpec((tmx, d), lambda i: (i, 0)),
                  pl.BlockSpec((tmx, a_w), lambda i: (i, 0)),
                  pl.BlockSpec((tmx, p_w), lambda i: (i, 0)),
                  pl.BlockSpec((halo_rows, p_w), lambda i: (jnp.maximum(i * hsub - 1, 0), 0)),
                  pl.BlockSpec((tmx, m_w), lambda i: (i, 0)),
                  pl.BlockSpec((mem_len, 2 * m_w), lambda i: (i // (seq // tmx), 0)),
                  _full(pool_w.shape), _full((1, p_w)), _full((d, d)), _full((1, d))],
        out_specs=[pl.BlockSpec((tmx, d), lambda i: (i, 0)),
                   pl.BlockSpec((tmx, d), lambda i: (i, 0))],
        out_shape=[jax.ShapeDtypeStruct((t, d), F32), jax.ShapeDtypeStruct((t, d), md)],
        compiler_params=_cparams("parallel"),
        name="mix_out",
    )(x2, a_out.reshape(t, a_w), pu, pu, qm, kvm, pool_w.astype(md), row(pool_s), w_out.astype(md), row(ffn_g))

    tmr = 256
    rshape = jax.ShapeDtypeStruct((PEER_HEADS, N_KEYS, t), F32)
    rspec = pl.BlockSpec((PEER_HEADS, N_KEYS, tmr), lambda i: (0, 0, i))
    r2, e2, nsel, csel = pl.pallas_call(
        functools.partial(_route_kernel, tm=tmr),
        grid=(t // tmr,),
        in_specs=[pl.BlockSpec((tmr, d), lambda i: (i, 0)),
                  _full(peer_wq.shape), _full(sk1.shape), _full(sk2.shape)],
        out_specs=[rspec, rspec, rspec, rspec],
        out_shape=[rshape, rshape, rshape, rshape],
        compiler_params=_cparams("parallel"),
        name="peer_route",
    )(hn, peer_wq.astype(md), sk1.astype(md), sk2.astype(md))

    tb, eb = 512, 512
    n_exp = peer_u.shape[0]
    gspec = pl.BlockSpec((PEER_HEADS, N_KEYS, tb), lambda ti, ei: (0, 0, ti))
    out = pl.pallas_call(
        functools.partial(_peer_kernel, tb=tb, eb=eb),
        grid=(t // tb, n_exp // eb),
        in_specs=[pl.BlockSpec((tb, d), lambda ti, ei: (ti, 0)),
                  pl.BlockSpec((eb, d), lambda ti, ei: (ei, 0)),
                  pl.BlockSpec((d, eb), lambda ti, ei: (0, ei)),
                  gspec, gspec, gspec, gspec,
                  pl.BlockSpec((tb, d), lambda ti, ei: (ti, 0))],
        out_specs=pl.BlockSpec((tb, d), lambda ti, ei: (ti, 0)),
        out_shape=jax.ShapeDtypeStruct((t, d), F32),
        scratch_shapes=[pltpu.VMEM((d, tb), F32), pltpu.VMEM((eb, tb), md)],
        compiler_params=_cparams("parallel", "arbitrary"),
        name="peer_experts",
    )(hn, peer_u.astype(md), peer_v.T.astype(md), r2, e2, nsel, csel, h_res)
    return out.reshape(b, seq, d)


def kernel(x, mem, attn_norm_gain, w_in, q_norm_gain, k_norm_gain, pool_w, pool_scale, mem_norm_gain,
           w_mem_kv, mq_norm_gain, mk_norm_gain, w_out, ffn_norm_gain, peer_w_q, peer_sub_keys_1,
           peer_sub_keys_2, peer_u, peer_v):
    for l in range(attn_norm_gain.shape[0]):
        x = _layer(x, mem, attn_norm_gain[l], w_in[l], q_norm_gain[l], k_norm_gain[l], pool_w[l],
                   pool_scale[l], mem_norm_gain[l], w_mem_kv[l], mq_norm_gain[l], mk_norm_gain[l],
                   w_out[l], ffn_norm_gain[l], peer_w_q[l], peer_sub_keys_1[l], peer_sub_keys_2[l],
                   peer_u[l], peer_v[l])
    return x
```

```python
import functools

import numpy as np
import jax
import jax.numpy as jnp
from jax import lax
from jax.experimental import pallas as pl
from jax.experimental.pallas import tpu as pltpu

F32 = jnp.float32
I32 = jnp.int32
MXU_DTYPE = jnp.bfloat16

LANES = 128
VMEM_LIMIT = 56 * 1024 * 1024

EPS = 1e-6
ROPE_THETA = 10000.0
CHUNK = 64
A_HEADS = 8
HEAD_DIM = 128
IDX_HEADS = 16
IDX_DIM = 64
TOP_K_MAX = 256
POOL_WINDOWS = (2, 4, 8, 16)
M_HEADS = 4
PEER_HEADS = 8
N_KEYS = 128
PEER_TOPK = 16
INT_MIN = -(2 ** 31)
NEG_BIG = -1e30


def _cparams(*sem):
    return pltpu.CompilerParams(dimension_semantics=sem, vmem_limit_bytes=VMEM_LIMIT)


def _rms(x, g):
    ms = jnp.mean(x * x, axis=-1, keepdims=True)
    return x * lax.rsqrt(ms + EPS) * g


def _dot(a, b):
    return jnp.dot(a, b, preferred_element_type=F32)


def _dot_nt(a, b):
    return lax.dot_general(a, b, (((1,), (1,)), ((), ())), preferred_element_type=F32)


def _lane_iota(shape):
    return lax.broadcasted_iota(I32, shape, len(shape) - 1)


def _qk_kernel(x_ref, g_ref, w_ref, hg_ref, cos_ref, sin_ref, o_ref, xn_ref):
    j = pl.program_id(1)

    @pl.when(j == 0)
    def _():
        xn_ref[...] = _rms(x_ref[...], g_ref[...]).astype(xn_ref.dtype)

    z = _dot(xn_ref[...], w_ref[...])

    cosf = cos_ref[...]
    sinf = sin_ref[...]
    hg = hg_ref[0]
    for h in range(A_HEADS):
        sl = slice(h * HEAD_DIM, (h + 1) * HEAD_DIM)
        n = _rms(z[:, sl], hg)
        o_ref[:, sl] = (n * cosf + pltpu.roll(n, HEAD_DIM // 2, 1) * sinf).astype(o_ref.dtype)


def _vt_kernel(x_ref, g_ref, wt_ref, o_ref):
    xn = _rms(x_ref[...], g_ref[...]).astype(MXU_DTYPE)
    o_ref[...] = _dot_nt(wt_ref[...], xn).astype(o_ref.dtype)


def _rope64(xb, cos4, sin4, first_half):
    rot = jnp.where(first_half, pltpu.roll(xb, LANES - IDX_DIM // 2, 1), pltpu.roll(xb, IDX_DIM // 2, 1))
    return xb * cos4 + rot * sin4


def _split3(v):
    hi = v.astype(MXU_DTYPE)
    r1 = v - hi.astype(F32)
    mid = r1.astype(MXU_DTYPE)
    lo = (r1 - mid.astype(F32)).astype(MXU_DTYPE)
    return hi, mid, lo


def _idx_kernel(x_ref, g_ref, wq_ref, ws_ref, e_ref, cos_ref, sin_ref, qi_ref, kd_ref, wt_ref):
    xn = _rms(x_ref[...], g_ref[...]).astype(MXU_DTYPE)
    zq = _dot(xn, wq_ref[...])
    zs = _dot(xn, ws_ref[...])
    wt_ref[...] = zs.T
    hi, mid, lo = _split3(zs)
    e = e_ref[...]
    wexp = _dot(hi, e) + _dot(mid, e) + _dot(lo, e)
    cos4 = cos_ref[...]
    sin4 = sin_ref[...]
    lane = _lane_iota(zs.shape)
    first_half = (lane & (IDX_DIM // 2)) == 0
    scale = (IDX_DIM ** -0.5) * (IDX_HEADS ** -0.5)
    for c in range(zq.shape[1] // LANES):
        sl = slice(c * LANES, (c + 1) * LANES)
        y = _rope64(zq[:, sl], cos4, sin4, first_half)
        qi_ref[:, sl] = (y * (wexp[:, sl] * scale)).astype(qi_ref.dtype)
    yk = _rope64(zs, cos4, sin4, first_half)
    kd_ref[...] = jnp.where(lane < IDX_DIM, yk, pltpu.roll(yk, IDX_DIM, 1)).astype(kd_ref.dtype)


def _plain_kernel(x_ref, g_ref, w_ref, o_ref):
    xn = _rms(x_ref[...], g_ref[...]).astype(MXU_DTYPE)
    o_ref[...] = _dot(xn, w_ref[...]).astype(o_ref.dtype)


def _headnorm_kernel(x_ref, g_ref, w_ref, hg_ref, o_ref, *, heads):
    xn = _rms(x_ref[...], g_ref[...]).astype(MXU_DTYPE)
    z = _dot(xn, w_ref[...])
    hg = hg_ref[...]
    for h in range(heads):
        sl = slice(h * HEAD_DIM, (h + 1) * HEAD_DIM)
        o_ref[:, sl] = _rms(z[:, sl], hg).astype(o_ref.dtype)


def _memkv_kernel(x_ref, g_ref, w_ref, hg_ref, o_ref):
    j = pl.program_id(0)
    xn = _rms(x_ref[...], g_ref[...]).astype(MXU_DTYPE)
    z = _dot(xn, w_ref[...])

    @pl.when(j == 0)
    def _():
        hg = hg_ref[...]
        for h in range(M_HEADS):
            sl = slice(h * HEAD_DIM, (h + 1) * HEAD_DIM)
            o_ref[:, sl] = _rms(z[:, sl], hg).astype(o_ref.dtype)

    @pl.when(j == 1)
    def _():
        o_ref[...] = z.astype(o_ref.dtype)


def _select_kernel(qi_ref, wt_ref, kd_ref, mask_ref, lhs_ref, lo_ref, hi_ref, key_ref, *, tq, tk, seq, top_k):
    qt = pl.program_id(1)
    t0 = qt * tq
    n_kt = (t0 + tq + tk - 1) // tk
    hgrp = 4
    rb = 128
    n_rb = tk // rb
    cr = 32

    lane = _lane_iota((tq, LANES))
    for h in range(IDX_HEADS):
        pair = qi_ref[:, (h // 2) * LANES:(h // 2 + 1) * LANES]
        own = (lane < IDX_DIM) if h % 2 == 0 else (lane >= IDX_DIM)
        lhs_ref[h * tq:(h + 1) * tq, :] = jnp.where(own, pair, jnp.zeros_like(pair))
    w = wt_ref[IDX_DIM:IDX_DIM + IDX_HEADS, :]
    lo_ref[...] = jnp.where(w > 0, 0.0, -jnp.inf).astype(F32)
    hi_ref[...] = jnp.where(w > 0, jnp.inf, 0.0).astype(F32)

    q_lim = ((t0 + _lane_iota((1, tq))) // CHUNK + 1) * CHUNK

    def rows(kt, r):
        return pl.ds(pl.multiple_of(kt * tk + r * rb, rb), rb)

    def score_tile(kt, carry):
        k0 = pl.multiple_of(kt * tk, tk)
        kd = kd_ref[pl.ds(k0, tk), :]
        accs = [jnp.zeros((rb, tq), F32) for _ in range(n_rb)]
        for g in range(IDX_HEADS // hgrp):
            raw = _dot_nt(kd, lhs_ref[g * hgrp * tq:(g + 1) * hgrp * tq, :])
            for hh in range(hgrp):
                h = g * hgrp + hh
                lo = lo_ref[h:h + 1, :]
                hi = hi_ref[h:h + 1, :]
                for r in range(n_rb):
                    blk = raw[r * rb:(r + 1) * rb, hh * tq:(hh + 1) * tq]
                    accs[r] = accs[r] + jnp.minimum(jnp.maximum(blk, lo), hi)
        for r in range(n_rb):
            bits = pltpu.bitcast(accs[r], I32)
            key = bits ^ ((bits >> 31) & 0x7FFFFFFF)
            srow = k0 + r * rb + lax.broadcasted_iota(I32, (rb, tq), 0)
            key_ref[rows(kt, r), :] = jnp.where(srow < q_lim, key, INT_MIN)
        return carry

    lax.fori_loop(0, n_kt, score_tile, 0)

    def bit_step(i, prefix):
        cand_u = prefix | lax.shift_left(jnp.int32(1), jnp.int32(31) - i)
        cand = cand_u ^ INT_MIN

        def count_tile(kt, cnt):
            for r in range(n_rb):
                ind = jnp.where(key_ref[rows(kt, r), :] >= cand, 1, 0)
                for c in range(rb // cr):
                    cnt = cnt + ind[c * cr:(c + 1) * cr]
            return cnt

        def count_pair(p, cnt):
            return count_tile(2 * p + 1, count_tile(2 * p, cnt))

        cnt = lax.fori_loop(0, n_kt // 2, count_pair, jnp.zeros((cr, tq), I32))
        cnt = lax.fori_loop(n_kt // 2 * 2, n_kt, count_tile, cnt)
        total = jnp.sum(cnt, axis=0, keepdims=True)
        return jnp.where(total >= top_k, cand_u, prefix)

    prefix = lax.fori_loop(0, 32, bit_step, jnp.zeros((1, tq), I32))
    tau = prefix ^ INT_MIN

    def write_tile(kt, carry):
        for r in range(n_rb):
            key = key_ref[rows(kt, r), :]
            sel = jnp.logical_and(key >= tau, key > INT_MIN)
            mask_ref[rows(kt, r), :] = sel.astype(mask_ref.dtype)
        return carry

    lax.fori_loop(0, n_kt, write_tile, 0)

    def zero_tile(kt, carry):
        mask_ref[pl.ds(pl.multiple_of(kt * tk, tk), tk), :] = jnp.zeros((tk, tq), mask_ref.dtype)
        return carry

    lax.fori_loop(n_kt, seq // tk, zero_tile, 0)


def _attn_kernel(qt_tab, kt_tab, q_ref, k_ref, vt_ref, mask_ref, o_ref, m_ref, l_ref, acc_ref, s_ref, *, tq, tk):
    step = pl.program_id(1)
    qt = qt_tab[step]
    kt = kt_tab[step]
    last_kt = ((qt + 1) * tq - 1) // tk

    @pl.when(kt == 0)
    def _():
        m_ref[...] = jnp.full(m_ref.shape, NEG_BIG, F32)
        l_ref[...] = jnp.zeros(l_ref.shape, F32)
        acc_ref[...] = jnp.zeros(acc_ref.shape, F32)

    bias = jnp.where(mask_ref[...].astype(I32) != 0, 0.0, -jnp.inf).astype(F32)
    hs = lambda h: slice(h * HEAD_DIM, (h + 1) * HEAD_DIM)

    def logits(h):
        s_ref[h % 2] = _dot_nt(k_ref[:, hs(h)], q_ref[:, hs(h)])

    logits(0)
    for h in range(A_HEADS):
        if h + 1 < A_HEADS:
            logits(h + 1)
        s = s_ref[h % 2] + bias
        m_prev = m_ref[h:h + 1, :]
        m_new = jnp.maximum(m_prev, jnp.max(s, axis=0, keepdims=True))
        alpha = jnp.exp2(m_prev - m_new)
        p = jnp.exp2(s - m_new)
        l_ref[h:h + 1, :] = alpha * l_ref[h:h + 1, :] + jnp.sum(p, axis=0, keepdims=True)
        acc_ref[hs(h), :] = alpha * acc_ref[hs(h), :] + _dot(vt_ref[hs(h), :], p.astype(MXU_DTYPE))
        m_ref[h:h + 1, :] = m_new

    @pl.when(kt == last_kt)
    def _():
        for h in range(A_HEADS):
            o_ref[:, hs(h)] = (acc_ref[hs(h), :] / l_ref[h:h + 1, :]).T.astype(o_ref.dtype)


def _mix_kernel(x_ref, a_ref, pu_ref, halo_ref, qm_ref, kvm_ref, pw_ref, ps_ref, wo_ref, fg_ref,
                h_ref, hn_ref, *, tm, seq):
    i = pl.program_id(0)
    tiles_per_seq = seq // tm
    pos0 = (i % tiles_per_seq) * tm
    halo_rows = halo_ref.shape[0]
    a_w = a_ref.shape[1]
    p_w = pu_ref.shape[1]
    grp = p_w // len(POOL_WINDOWS)

    halo = jnp.where(pos0 > 0, halo_ref[...], 0.0)
    ext = jnp.concatenate([halo, pu_ref[...]], axis=0)
    pos = pos0 + lax.broadcasted_iota(I32, (tm, grp), 0)
    out = x_ref[...] + _dot(a_ref[...], wo_ref[0:a_w, :])
    run = ext
    width = 1
    for g, w in enumerate(POOL_WINDOWS):
        while width < w:
            run = run + pltpu.roll(run, width, 0)
            width *= 2
        sl = slice(g * grp, (g + 1) * grp)
        wsum = run[halo_rows:, sl]
        cnt = jnp.minimum(pos + 1, w).astype(F32)
        d = wsum / cnt - pu_ref[:, sl]
        pg = _dot(d.astype(MXU_DTYPE), pw_ref[g]) * ps_ref[:, sl]
        out = out + _dot(pg.astype(MXU_DTYPE), wo_ref[a_w + g * grp:a_w + (g + 1) * grp, :])

    m_w = qm_ref.shape[1]
    for h in range(M_HEADS):
        sl = slice(h * HEAD_DIM, (h + 1) * HEAD_DIM)
        s = _dot_nt(qm_ref[:, sl], kvm_ref[:, sl])
        p = jnp.exp(s - jnp.max(s, axis=1, keepdims=True))
        o = _dot(p.astype(MXU_DTYPE), kvm_ref[:, m_w + h * HEAD_DIM:m_w + (h + 1) * HEAD_DIM])
        o = o / jnp.sum(p, axis=1, keepdims=True)
        r0 = a_w + p_w + h * HEAD_DIM
        out = out + _dot(o.astype(MXU_DTYPE), wo_ref[r0:r0 + HEAD_DIM, :])

    h_ref[...] = out
    hn_ref[...] = _rms(out, fg_ref[...]).astype(hn_ref.dtype)


def _top16_rows(s, row):
    rank = jnp.full(s.shape, float(PEER_TOPK), F32)
    big = jnp.int32(s.shape[0])
    vals = []
    for k in range(PEER_TOPK):
        m = jnp.max(s, axis=0, keepdims=True)
        first = jnp.min(jnp.where(s == m, row, big), axis=0, keepdims=True)
        hit = row == first
        rank = jnp.where(hit, float(k), rank)
        s = jnp.where(hit, -jnp.inf, s)
        vals.append(m)
    return jnp.concatenate(vals, axis=0), rank


def _route_kernel(hn_ref, wq_ref, k1_ref, k2_ref, r2_ref, e2_ref, n_ref, c_ref, *, tm):
    q = _dot(hn_ref[...], wq_ref[...]).astype(MXU_DTYPE)
    row = lax.broadcasted_iota(I32, (N_KEYS, tm), 0)
    sub = lax.broadcasted_iota(I32, (8, tm), 0)
    half = N_KEYS
    for h in range(PEER_HEADS):
        s1 = _dot_nt(k1_ref[...], q[:, h * 2 * half:h * 2 * half + half])
        s2 = _dot_nt(k2_ref[...], q[:, h * 2 * half + half:(h + 1) * 2 * half])
        v1, rank1 = _top16_rows(s1, row)
        v2, rank2 = _top16_rows(s2, row)

        blocks, flats = [], []
        for half_a in range(2):
            blocks.append(v1[half_a * 8:(half_a + 1) * 8] + v2[0:1])
            flats.append((sub + half_a * 8) * PEER_TOPK)
        for b in range(1, 8):
            blk = v1[0:8] + v2[b:b + 1]
            blocks.append(jnp.where(sub < PEER_TOPK // (b + 1), blk, -jnp.inf))
            flats.append(sub * PEER_TOPK + b)
        blocks.append(v1[0:1] + v2[8:16])
        flats.append(sub + 8)
        cand = jnp.concatenate(blocks, axis=0)
        flat = jnp.concatenate(flats, axis=0)
        cand0 = cand
        sel = jnp.zeros(cand.shape, jnp.bool_)
        for _ in range(PEER_TOPK):
            m = jnp.max(cand, axis=0, keepdims=True)
            first = jnp.min(jnp.where(cand == m, flat, PEER_TOPK * PEER_TOPK), axis=0, keepdims=True)
            hit = flat == first
            sel = jnp.logical_or(sel, hit)
            cand = jnp.where(hit, -jnp.inf, cand)
        top = v1[0:1] + v2[0:1]
        z = jnp.sum(jnp.where(sel, jnp.exp(cand0 - top), 0.0), axis=0, keepdims=True)
        self32 = sel.astype(F32)
        nb_lo = self32[0:8]
        for blk in range(2, 9):
            nb_lo = nb_lo + self32[blk * 8:(blk + 1) * 8]
        tail = jnp.sum(self32[72:80], axis=0, keepdims=True)
        nb_lo = nb_lo + jnp.where(sub == 0, tail, 0.0)
        nb = jnp.concatenate([nb_lo, self32[8:16]], axis=0)

        n_dense = jnp.zeros((N_KEYS, tm), F32)
        for a in range(PEER_TOPK):
            n_dense = jnp.where(rank1 == float(a), nb[a:a + 1], n_dense)
        r2_ref[h] = rank2
        e2_ref[h] = jnp.exp(s2 - v2[0:1])
        n_ref[h] = n_dense
        c_ref[h] = jnp.exp(s1 - v1[0:1]) / z


def _gelu(x):
    return 0.5 * x * (1.0 + lax.erf(x * (2.0 ** -0.5)))


def _peer_kernel(hn_ref, u_ref, vt_ref, r2_ref, e2_ref, n_ref, c_ref, h_ref, o_ref, acc_ref, at_ref, *, tb, eb):
    ei = pl.program_id(1)

    @pl.when(ei == 0)
    def _():
        acc_ref[...] = jnp.zeros(acc_ref.shape, F32)

    ht = _dot_nt(u_ref[...], hn_ref[...])
    ipb = eb // N_KEYS
    sub = 8
    base = pl.multiple_of((ei * ipb) // sub * sub, sub)
    off = (ei * ipb) % sub
    for il in range(ipb):
        for tcol in range(tb // LANES):
            cs = slice(tcol * LANES, (tcol + 1) * LANES)
            gate = jnp.zeros((N_KEYS, LANES), F32)
            for h in range(PEER_HEADS):
                n8 = n_ref[h, pl.ds(base, sub), cs]
                c8 = c_ref[h, pl.ds(base, sub), cs]
                nrow = n8[il:il + 1]
                crow = c8[il:il + 1]
                for o in range(ipb, sub, ipb):
                    nrow = jnp.where(off == o, n8[o + il:o + il + 1], nrow)
                    crow = jnp.where(off == o, c8[o + il:o + il + 1], crow)
                gate = gate + jnp.where(r2_ref[h, :, cs] < nrow, e2_ref[h, :, cs], 0.0) * crow
            a = gate * _gelu(ht[il * N_KEYS:(il + 1) * N_KEYS, cs])
            at_ref[il * N_KEYS:(il + 1) * N_KEYS, cs] = a.astype(at_ref.dtype)
    acc_ref[...] += _dot(vt_ref[...], at_ref[...])

    @pl.when(ei == pl.num_programs(1) - 1)
    def _():
        o_ref[...] = h_ref[...] + acc_ref[...].T


def _rope_tables(seq):
    pos = jnp.arange(seq, dtype=F32)[:, None]

    def table(half):
        inv = ROPE_THETA ** (-jnp.arange(half, dtype=F32) / half)
        ang = pos * inv[None, :]
        return jnp.cos(ang), jnp.sin(ang)

    c64, s64 = table(HEAD_DIM // 2)
    c32, s32 = table(IDX_DIM // 2)
    cos_a = jnp.concatenate([c64, c64], axis=1)
    sin_a = jnp.concatenate([-s64, s64], axis=1)
    cos_i = jnp.concatenate([c32, c32, c32, c32], axis=1)
    sin_i = jnp.concatenate([-s32, s32, -s32, s32], axis=1)
    return cos_a, sin_a, cos_i, sin_i


def _attn_steps(seq, tq, tk):
    qts, kts = [], []
    for qt in range(seq // tq):
        for kt in range(((qt + 1) * tq - 1) // tk + 1):
            qts.append(qt)
            kts.append(kt)
    return np.asarray(qts, np.int32), np.asarray(kts, np.int32)


def _full(shape):
    return pl.BlockSpec(shape, lambda *_: (0,) * len(shape))


def _layer(x, mem, attn_g, w_in, q_g, k_g, pool_w, pool_s, mem_g, w_mem_kv, mq_g, mk_g, w_out,
           ffn_g, peer_wq, sk1, sk2, peer_u, peer_v):
    b, seq, d = x.shape
    t = b * seq
    mem_len = mem.shape[1]
    a_w = A_HEADS * HEAD_DIM
    i_w = IDX_HEADS * IDX_DIM
    p_w = pool_w.shape[0] * pool_w.shape[1]
    m_w = M_HEADS * HEAD_DIM
    top_k = min(TOP_K_MAX, seq // 4)
    md = MXU_DTYPE

    x2 = x.reshape(t, d)
    row = lambda v: v.reshape(1, -1).astype(F32)
    cos_a, sin_a, cos_i, sin_i = _rope_tables(seq)

    o = 0
    w_qk = w_in[:, o:o + 2 * a_w].astype(md); o += 2 * a_w
    w_vt = w_in[:, o:o + a_w].T.astype(md); o += a_w
    w_qi = w_in[:, o:o + i_w].astype(md); o += i_w
    w_small = jnp.pad(w_in[:, o:o + IDX_DIM + IDX_HEADS], ((0, 0), (0, LANES - IDX_DIM - IDX_HEADS))).astype(md)
    o += IDX_DIM + IDX_HEADS
    w_pu = w_in[:, o:o + p_w].astype(md); o += p_w
    w_qm = w_in[:, o:o + m_w].astype(md)

    tm = min(1024, seq)
    n_rt = t // tm
    pos_blk = lambda i: i % (seq // tm)
    hg = jnp.stack([row(q_g) * (HEAD_DIM ** -0.5 * np.log2(np.e)), row(k_g)])
    qk = pl.pallas_call(
        _qk_kernel,
        grid=(n_rt, 2),
        in_specs=[pl.BlockSpec((tm, d), lambda i, j: (i, 0)),
                  pl.BlockSpec((1, d), lambda i, j: (0, 0)),
                  pl.BlockSpec((d, a_w), lambda i, j: (0, j)),
                  pl.BlockSpec((1, 1, HEAD_DIM), lambda i, j: (j, 0, 0)),
                  pl.BlockSpec((tm, HEAD_DIM), lambda i, j: (pos_blk(i), 0)),
                  pl.BlockSpec((tm, HEAD_DIM), lambda i, j: (pos_blk(i), 0))],
        out_specs=pl.BlockSpec((tm, a_w), lambda i, j: (i, j)),
        out_shape=jax.ShapeDtypeStruct((t, 2 * a_w), md),
        scratch_shapes=[pltpu.VMEM((tm, d), md)],
        compiler_params=_cparams("parallel", "arbitrary"),
        name="qk_proj",
    )(x2, row(attn_g), w_qk, hg, cos_a, sin_a)

    tmi = min(512, seq)
    vt = pl.pallas_call(
        _vt_kernel,
        grid=(t // tmi,),
        in_specs=[pl.BlockSpec((tmi, d), lambda i: (i, 0)), _full((1, d)), _full((a_w, d))],
        out_specs=pl.BlockSpec((a_w, tmi), lambda i: (0, i)),
        out_shape=jax.ShapeDtypeStruct((a_w, t), md),
        compiler_params=_cparams("parallel"),
        name="vt_proj",
    )(x2, row(attn_g), w_vt)

    expand = np.zeros((LANES, i_w), np.float32)
    for h in range(IDX_HEADS):
        expand[IDX_DIM + h, h * IDX_DIM:(h + 1) * IDX_DIM] = 1.0
    qi, kd, wt = pl.pallas_call(
        _idx_kernel,
        grid=(t // tmi,),
        in_specs=[pl.BlockSpec((tmi, d), lambda i: (i, 0)),
                  _full((1, d)), _full((d, i_w)), _full((d, LANES)), _full((LANES, i_w)),
                  pl.BlockSpec((tmi, LANES), lambda i: (i % (seq // tmi), 0)),
                  pl.BlockSpec((tmi, LANES), lambda i: (i % (seq // tmi), 0))],
        out_specs=[pl.BlockSpec((tmi, i_w), lambda i: (i, 0)),
                   pl.BlockSpec((tmi, LANES), lambda i: (i, 0)),
                   pl.BlockSpec((LANES, tmi), lambda i: (0, i))],
        out_shape=[jax.ShapeDtypeStruct((t, i_w), md),
                   jax.ShapeDtypeStruct((t, LANES), md),
                   jax.ShapeDtypeStruct((LANES, t), F32)],
        compiler_params=_cparams("parallel"),
        name="idx_proj",
    )(x2, row(attn_g), w_qi, w_small, jnp.asarray(expand, md), cos_i, sin_i)

    pu = pl.pallas_call(
        _plain_kernel,
        grid=(t // tmi,),
        in_specs=[pl.BlockSpec((tmi, d), lambda i: (i, 0)), _full((1, d)), _full((d, p_w))],
        out_specs=pl.BlockSpec((tmi, p_w), lambda i: (i, 0)),
        out_shape=jax.ShapeDtypeStruct((t, p_w), F32),
        compiler_params=_cparams("parallel"),
        name="pool_proj",
    )(x2, row(attn_g), w_pu)

    qm = pl.pallas_call(
        functools.partial(_headnorm_kernel, heads=M_HEADS),
        grid=(t // tmi,),
        in_specs=[pl.BlockSpec((tmi, d), lambda i: (i, 0)), _full((1, d)), _full((d, m_w)), _full((1, HEAD_DIM))],
        out_specs=pl.BlockSpec((tmi, m_w), lambda i: (i, 0)),
        out_shape=jax.ShapeDtypeStruct((t, m_w), md),
        compiler_params=_cparams("parallel"),
        name="memq_proj",
    )(x2, row(attn_g), w_qm, row(mq_g) * (HEAD_DIM ** -0.5))

    tmem = b * mem_len
    kvm = pl.pallas_call(
        _memkv_kernel,
        grid=(2,),
        in_specs=[_full((tmem, d)), _full((1, d)),
                  pl.BlockSpec((d, m_w), lambda j: (0, j)), _full((1, HEAD_DIM))],
        out_specs=pl.BlockSpec((tmem, m_w), lambda j: (0, j)),
        out_shape=jax.ShapeDtypeStruct((tmem, 2 * m_w), md),
        compiler_params=_cparams("arbitrary"),
        name="memkv_proj",
    )(mem.reshape(tmem, d), row(mem_g), w_mem_kv.astype(md), row(mk_g))

    tq, tks = 256, 512
    n_qt = seq // tq
    mask = pl.pallas_call(
        functools.partial(_select_kernel, tq=tq, tk=tks, seq=seq, top_k=top_k),
        grid=(b, n_qt),
        in_specs=[pl.BlockSpec((None, tq, i_w), lambda bi, qt: (bi, qt, 0)),
                  pl.BlockSpec((LANES, tq), lambda bi, qt: (0, bi * n_qt + qt)),
                  pl.BlockSpec((None, seq, LANES), lambda bi, qt: (bi, 0, 0))],
        out_specs=pl.BlockSpec((None, seq, tq), lambda bi, qt: (bi, 0, qt)),
        out_shape=jax.ShapeDtypeStruct((b, seq, seq), jnp.int8),
        scratch_shapes=[pltpu.VMEM((IDX_HEADS * tq, LANES), md),
                        pltpu.VMEM((IDX_HEADS, tq), F32),
                        pltpu.VMEM((IDX_HEADS, tq), F32),
                        pltpu.VMEM((seq, tq), I32)],
        compiler_params=_cparams("parallel", "arbitrary"),
        name="dsa_select",
    )(qi.reshape(b, seq, i_w), wt, kd.reshape(b, seq, LANES))

    tqa, tka = min(512, seq), min(1024, seq)
    n_kta = seq // tka
    qts, kts = _attn_steps(seq, tqa, tka)
    qk3 = qk.reshape(b, seq, 2 * a_w)
    a_out = pl.pallas_call(
        functools.partial(_attn_kernel, tq=tqa, tk=tka),
        grid_spec=pltpu.PrefetchScalarGridSpec(
            num_scalar_prefetch=2,
            grid=(b, len(qts)),
            in_specs=[pl.BlockSpec((None, tqa, a_w), lambda bi, s, qt, kt: (bi, qt[s], 0)),
                      pl.BlockSpec((None, tka, a_w), lambda bi, s, qt, kt: (bi, kt[s], 1)),
                      pl.BlockSpec((a_w, tka), lambda bi, s, qt, kt: (0, bi * n_kta + kt[s])),
                      pl.BlockSpec((None, tka, tqa), lambda bi, s, qt, kt: (bi, kt[s], qt[s]))],
            out_specs=pl.BlockSpec((None, tqa, a_w), lambda bi, s, qt, kt: (bi, qt[s], 0)),
            scratch_shapes=[pltpu.VMEM((A_HEADS, tqa), F32),
                            pltpu.VMEM((A_HEADS, tqa), F32),
                            pltpu.VMEM((a_w, tqa), F32),
                            pltpu.VMEM((2, tka, tqa), F32)]),
        out_shape=jax.ShapeDtypeStruct((b, seq, a_w), md),
        compiler_params=_cparams("parallel", "arbitrary"),
        name="dsa_attn",
    )(jnp.asarray(qts), jnp.asarray(kts), qk3, qk3, vt, mask)

    tmx = min(512, seq)
    halo_rows = POOL_WINDOWS[-1]
    hsub = tmx // halo_rows
    h_res, hn = pl.pallas_call(
        functools.partial(_mix_kernel, tm=tmx, seq=seq),
        grid=(t // tmx,),
        in_specs=[pl.BlockSpec((tmx, d), lambda i: (i, 0)),
                  pl.BlockSpec((tmx, a_w), lambda i: (i, 0)),
                  pl.BlockSpec((tmx, p_w), lambda i: (i, 0)),
                  pl.BlockSpec((halo_rows, p_w), lambda i: (jnp.maximum(i * hsub - 1, 0), 0)),
                  pl.BlockSpec((tmx, m_w), lambda i: (i, 0)),
                  pl.BlockSpec((mem_len, 2 * m_w), lambda i: (i // (seq // tmx), 0)),
                  _full(pool_w.shape), _full((1, p_w)), _full((d, d)), _full((1, d))],
        out_specs=[pl.BlockSpec((tmx, d), lambda i: (i, 0)),
                   pl.BlockSpec((tmx, d), lambda i: (i, 0))],
        out_shape=[jax.ShapeDtypeStruct((t, d), F32), jax.ShapeDtypeStruct((t, d), md)],
        compiler_params=_cparams("parallel"),
        name="mix_out",
    )(x2, a_out.reshape(t, a_w), pu, pu, qm, kvm, pool_w.astype(md), row(pool_s), w_out.astype(md), row(ffn_g))

    tmr = 256
    rshape = jax.ShapeDtypeStruct((PEER_HEADS, N_KEYS, t), F32)
    rspec = pl.BlockSpec((PEER_HEADS, N_KEYS, tmr), lambda i: (0, 0, i))
    r2, e2, nsel, csel = pl.pallas_call(
        functools.partial(_route_kernel, tm=tmr),
        grid=(t // tmr,),
        in_specs=[pl.BlockSpec((tmr, d), lambda i: (i, 0)),
                  _full(peer_wq.shape), _full(sk1.shape), _full(sk2.shape)],
        out_specs=[rspec, rspec, rspec, rspec],
        out_shape=[rshape, rshape, rshape, rshape],
        compiler_params=_cparams("parallel"),
        name="peer_route",
    )(hn, peer_wq.astype(md), sk1.astype(md), sk2.astype(md))

    tb, eb = 512, 512
    n_exp = peer_u.shape[0]
    gspec = pl.BlockSpec((PEER_HEADS, N_KEYS, tb), lambda ti, ei: (0, 0, ti))
    out = pl.pallas_call(
        functools.partial(_peer_kernel, tb=tb, eb=eb),
        grid=(t // tb, n_exp // eb),
        in_specs=[pl.BlockSpec((tb, d), lambda ti, ei: (ti, 0)),
                  pl.BlockSpec((eb, d), lambda ti, ei: (ei, 0)),
                  pl.BlockSpec((d, eb), lambda ti, ei: (0, ei)),
                  gspec, gspec, gspec, gspec,
                  pl.BlockSpec((tb, d), lambda ti, ei: (ti, 0))],
        out_specs=pl.BlockSpec((tb, d), lambda ti, ei: (ti, 0)),
        out_shape=jax.ShapeDtypeStruct((t, d), F32),
        scratch_shapes=[pltpu.VMEM((d, tb), F32), pltpu.VMEM((eb, tb), md)],
        compiler_params=_cparams("parallel", "arbitrary"),
        name="peer_experts",
    )(hn, peer_u.astype(md), peer_v.T.astype(md), r2, e2, nsel, csel, h_res)
    return out.reshape(b, seq, d)


def kernel(x, mem, attn_norm_gain, w_in, q_norm_gain, k_norm_gain, pool_w, pool_scale, mem_norm_gain,
           w_mem_kv, mq_norm_gain, mk_norm_gain, w_out, ffn_norm_gain, peer_w_q, peer_sub_keys_1,
           peer_sub_keys_2, peer_u, peer_v):
    for l in range(attn_norm_gain.shape[0]):
        x = _layer(x, mem, attn_norm_gain[l], w_in[l], q_norm_gain[l], k_norm_gain[l], pool_w[l],
                   pool_scale[l], mem_norm_gain[l], w_mem_kv[l], mq_norm_gain[l], mk_norm_gain[l],
                   w_out[l], ffn_norm_gain[l], peer_w_q[l], peer_sub_keys_1[l], peer_sub_keys_2[l],
                   peer_u[l], peer_v[l])
    return x
```

```python
import functools

import numpy as np
import jax
import jax.numpy as jnp
from jax import lax
from jax.experimental import pallas as pl
from jax.experimental.pallas import tpu as pltpu

F32 = jnp.float32
I32 = jnp.int32
MXU_DTYPE = jnp.bfloat16

LANES = 128
VMEM_LIMIT = 56 * 1024 * 1024

EPS = 1e-6
ROPE_THETA = 10000.0
CHUNK = 64
A_HEADS = 8
HEAD_DIM = 128
IDX_HEADS = 16
IDX_DIM = 64
TOP_K_MAX = 256
POOL_WINDOWS = (2, 4, 8, 16)
M_HEADS = 4
PEER_HEADS = 8
N_KEYS = 128
PEER_TOPK = 16
INT_MIN = -(2 ** 31)
NEG_BIG = -1e30


def _cparams(*sem):
    return pltpu.CompilerParams(dimension_semantics=sem, vmem_limit_bytes=VMEM_LIMIT)


def _rms(x, g):
    ms = jnp.mean(x * x, axis=-1, keepdims=True)
    return x * lax.rsqrt(ms + EPS) * g


def _dot(a, b):
    return jnp.dot(a, b, preferred_element_type=F32)


def _dot_nt(a, b):
    return lax.dot_general(a, b, (((1,), (1,)), ((), ())), preferred_element_type=F32)


def _lane_iota(shape):
    return lax.broadcasted_iota(I32, shape, len(shape) - 1)


def _qk_kernel(x_ref, g_ref, w_ref, hg_ref, cos_ref, sin_ref, o_ref, xn_ref):
    j = pl.program_id(1)

    @pl.when(j == 0)
    def _():
        xn_ref[...] = _rms(x_ref[...], g_ref[...]).astype(xn_ref.dtype)

    z = _dot(xn_ref[...], w_ref[...])

    cosf = cos_ref[...]
    sinf = sin_ref[...]
    hg = hg_ref[0]
    for h in range(A_HEADS):
        sl = slice(h * HEAD_DIM, (h + 1) * HEAD_DIM)
        n = _rms(z[:, sl], hg)
        o_ref[:, sl] = (n * cosf + pltpu.roll(n, HEAD_DIM // 2, 1) * sinf).astype(o_ref.dtype)


def _vt_kernel(x_ref, g_ref, wt_ref, o_ref):
    xn = _rms(x_ref[...], g_ref[...]).astype(MXU_DTYPE)
    o_ref[...] = _dot_nt(wt_ref[...], xn).astype(o_ref.dtype)


def _rope64(xb, cos4, sin4, first_half):
    rot = jnp.where(first_half, pltpu.roll(xb, LANES - IDX_DIM // 2, 1), pltpu.roll(xb, IDX_DIM // 2, 1))
    return xb * cos4 + rot * sin4


def _split3(v):
    hi = v.astype(MXU_DTYPE)
    r1 = v - hi.astype(F32)
    mid = r1.astype(MXU_DTYPE)
    lo = (r1 - mid.astype(F32)).astype(MXU_DTYPE)
    return hi, mid, lo


def _idx_kernel(x_ref, g_ref, wq_ref, ws_ref, e_ref, cos_ref, sin_ref, qi_ref, kd_ref, wt_ref):
    xn = _rms(x_ref[...], g_ref[...]).astype(MXU_DTYPE)
    zq = _dot(xn, wq_ref[...])
    zs = _dot(xn, ws_ref[...])
    wt_ref[...] = zs.T
    hi, mid, lo = _split3(zs)
    e = e_ref[...]
    wexp = _dot(hi, e) + _dot(mid, e) + _dot(lo, e)
    cos4 = cos_ref[...]
    sin4 = sin_ref[...]
    lane = _lane_iota(zs.shape)
    first_half = (lane & (IDX_DIM // 2)) == 0
    scale = (IDX_DIM ** -0.5) * (IDX_HEADS ** -0.5)
    for c in range(zq.shape[1] // LANES):
        sl = slice(c * LANES, (c + 1) * LANES)
        y = _rope64(zq[:, sl], cos4, sin4, first_half)
        qi_ref[:, sl] = (y * (wexp[:, sl] * scale)).astype(qi_ref.dtype)
    yk = _rope64(zs, cos4, sin4, first_half)
    kd_ref[...] = jnp.where(lane < IDX_DIM, yk, pltpu.roll(yk, IDX_DIM, 1)).astype(kd_ref.dtype)


def _plain_kernel(x_ref, g_ref, w_ref, o_ref):
    xn = _rms(x_ref[...], g_ref[...]).astype(MXU_DTYPE)
    o_ref[...] = _dot(xn, w_ref[...]).astype(o_ref.dtype)


def _headnorm_kernel(x_ref, g_ref, w_ref, hg_ref, o_ref, *, heads):
    xn = _rms(x_ref[...], g_ref[...]).astype(MXU_DTYPE)
    z = _dot(xn, w_ref[...])
    hg = hg_ref[...]
    for h in range(heads):
        sl = slice(h * HEAD_DIM, (h + 1) * HEAD_DIM)
        o_ref[:, sl] = _rms(z[:, sl], hg).astype(o_ref.dtype)


def _memkv_kernel(x_ref, g_ref, w_ref, hg_ref, o_ref):
    j = pl.program_id(0)
    xn = _rms(x_ref[...], g_ref[...]).astype(MXU_DTYPE)
    z = _dot(xn, w_ref[...])

    @pl.when(j == 0)
    def _():
        hg = hg_ref[...]
        for h in range(M_HEADS):
            sl = slice(h * HEAD_DIM, (h + 1) * HEAD_DIM)
            o_ref[:, sl] = _rms(z[:, sl], hg).astype(o_ref.dtype)

    @pl.when(j == 1)
    def _():
        o_ref[...] = z.astype(o_ref.dtype)


def _select_kernel(qi_ref, wt_ref, kd_ref, mask_ref, lhs_ref, lo_ref, hi_ref, key_ref, *, tq, tk, seq, top_k):
    qt = pl.program_id(1)
    t0 = qt * tq
    n_kt = (t0 + tq + tk - 1) // tk
    hgrp = 4
    rb = 128
    n_rb = tk // rb
    cr = 32

    lane = _lane_iota((tq, LANES))
    for h in range(IDX_HEADS):
        pair = qi_ref[:, (h // 2) * LANES:(h // 2 + 1) * LANES]
        own = (lane < IDX_DIM) if h % 2 == 0 else (lane >= IDX_DIM)
        lhs_ref[h * tq:(h + 1) * tq, :] = jnp.where(own, pair, jnp.zeros_like(pair))
    w = wt_ref[IDX_DIM:IDX_DIM + IDX_HEADS, :]
    lo_ref[...] = jnp.where(w > 0, 0.0, -jnp.inf).astype(F32)
    hi_ref[...] = jnp.where(w > 0, jnp.inf, 0.0).astype(F32)

    q_lim = ((t0 + _lane_iota((1, tq))) // CHUNK + 1) * CHUNK

    def rows(kt, r):
        return pl.ds(pl.multiple_of(kt * tk + r * rb, rb), rb)

    def score_tile(kt, carry):
        k0 = pl.multiple_of(kt * tk, tk)
        kd = kd_ref[pl.ds(k0, tk), :]
        accs = [jnp.zeros((rb, tq), F32) for _ in range(n_rb)]
        for g in range(IDX_HEADS // hgrp):
            raw = _dot_nt(kd, lhs_ref[g * hgrp * tq:(g + 1) * hgrp * tq, :])
            for hh in range(hgrp):
                h = g * hgrp + hh
                lo = lo_ref[h:h + 1, :]
                hi = hi_ref[h:h + 1, :]
                for r in range(n_rb):
                    blk = raw[r * rb:(r + 1) * rb, hh * tq:(hh + 1) * tq]
                    accs[r] = accs[r] + jnp.minimum(jnp.maximum(blk, lo), hi)
        for r in range(n_rb):
            bits = pltpu.bitcast(accs[r], I32)
            key = bits ^ ((bits >> 31) & 0x7FFFFFFF)
            key_ref[rows(kt, r), :] = jnp.where(k0 + r * rb + lax.broadcasted_iota(I32, (rb, tq), 0) < q_lim, key, INT_MIN)
        return carry

    lax.fori_loop(0, n_kt, score_tile, 0)

    def srow(kt, r):
        return kt * tk + r * rb + lax.broadcasted_iota(I32, (rb, tq), 0)

    def count_keys(ind_fn):
        def tile(kt, cnt):
            for r in range(n_rb):
                ind = ind_fn(key_ref[rows(kt, r), :], kt, r)
                for c in range(rb // cr):
                    cnt = cnt + ind[c * cr:(c + 1) * cr]
            return cnt

        def pair(p, cnt):
            return tile(2 * p + 1, tile(2 * p, cnt))

        cnt = lax.fori_loop(0, n_kt // 2, pair, jnp.zeros((cr, tq), I32))
        cnt = lax.fori_loop(n_kt // 2 * 2, n_kt, tile, cnt)
        return jnp.sum(cnt, axis=0, keepdims=True)

    def bit_step(i, carry):
        prefix, n_ge = carry
        cand_u = prefix | lax.shift_left(jnp.int32(1), jnp.int32(31) - i)
        cand = cand_u ^ INT_MIN
        total = count_keys(lambda key, kt, r: jnp.where(key >= cand, 1, 0))
        keep = total >= top_k
        return jnp.where(keep, cand_u, prefix), jnp.where(keep, total, n_ge)

    zero_row = jnp.zeros((1, tq), I32)
    prefix, n_ge = lax.fori_loop(0, 32, bit_step, (zero_row, zero_row))
    tau = prefix ^ INT_MIN

    def write_mask(sel_fn):
        def write_tile(kt, carry):
            for r in range(n_rb):
                key = key_ref[rows(kt, r), :]
                sel = jnp.logical_and(sel_fn(key, kt, r), key > INT_MIN)
                mask_ref[rows(kt, r), :] = sel.astype(mask_ref.dtype)
            return carry

        lax.fori_loop(0, n_kt, write_tile, 0)

    tied = n_ge > top_k
    any_tied = jnp.max(tied.astype(I32))

    @pl.when(any_tied == 0)
    def _():
        write_mask(lambda key, kt, r: key >= tau)

    @pl.when(any_tied > 0)
    def _():
        n_gt = count_keys(lambda key, kt, r: jnp.where(key > tau, 1, 0))
        need = jnp.where(tied, top_k - n_gt, 2 ** 31 - 1)
        idx_bits = max(1, (seq - 1).bit_length())

        def idx_step(i, p):
            cand = p | lax.shift_left(jnp.int32(1), jnp.int32(idx_bits - 1) - i)
            before = count_keys(lambda key, kt, r: jnp.where(jnp.logical_and(key == tau, srow(kt, r) < cand), 1, 0))
            return jnp.where(before < need, cand, p)

        cut = lax.fori_loop(0, idx_bits, idx_step, zero_row)
        write_mask(lambda key, kt, r: jnp.logical_or(key > tau, jnp.logical_and(key == tau, srow(kt, r) <= cut)))

    def zero_tile(kt, carry):
        mask_ref[pl.ds(pl.multiple_of(kt * tk, tk), tk), :] = jnp.zeros((tk, tq), mask_ref.dtype)
        return carry

    lax.fori_loop(n_kt, seq // tk, zero_tile, 0)


def _attn_kernel(qt_tab, kt_tab, q_ref, k_ref, vt_ref, mask_ref, o_ref, m_ref, l_ref, acc_ref, s_ref, *, tq, tk):
    step = pl.program_id(1)
    qt = qt_tab[step]
    kt = kt_tab[step]
    last_kt = ((qt + 1) * tq - 1) // tk

    @pl.when(kt == 0)
    def _():
        m_ref[...] = jnp.full(m_ref.shape, NEG_BIG, F32)
        l_ref[...] = jnp.zeros(l_ref.shape, F32)
        acc_ref[...] = jnp.zeros(acc_ref.shape, F32)

    bias = jnp.where(mask_ref[...].astype(I32) != 0, 0.0, -jnp.inf).astype(F32)
    hs = lambda h: slice(h * HEAD_DIM, (h + 1) * HEAD_DIM)

    def logits(h):
        s_ref[h % 2] = _dot_nt(k_ref[:, hs(h)], q_ref[:, hs(h)])

    logits(0)
    for h in range(A_HEADS):
        if h + 1 < A_HEADS:
            logits(h + 1)
        s = s_ref[h % 2] + bias
        m_prev = m_ref[h:h + 1, :]
        m_new = jnp.maximum(m_prev, jnp.max(s, axis=0, keepdims=True))
        alpha = jnp.exp2(m_prev - m_new)
        p = jnp.exp2(s - m_new)
        l_ref[h:h + 1, :] = alpha * l_ref[h:h + 1, :] + jnp.sum(p, axis=0, keepdims=True)
        acc_ref[hs(h), :] = alpha * acc_ref[hs(h), :] + _dot(vt_ref[hs(h), :], p.astype(MXU_DTYPE))
        m_ref[h:h + 1, :] = m_new

    @pl.when(kt == last_kt)
    def _():
        for h in range(A_HEADS):
            o_ref[:, hs(h)] = (acc_ref[hs(h), :] / l_ref[h:h + 1, :]).T.astype(o_ref.dtype)


def _mix_kernel(x_ref, a_ref, pu_ref, halo_ref, qm_ref, kvm_ref, pw_ref, ps_ref, wo_ref, fg_ref,
                h_ref, hn_ref, *, tm, seq):
    i = pl.program_id(0)
    tiles_per_seq = seq // tm
    pos0 = (i % tiles_per_seq) * tm
    halo_rows = halo_ref.shape[0]
    a_w = a_ref.shape[1]
    p_w = pu_ref.shape[1]
    grp = p_w // len(POOL_WINDOWS)

    halo = jnp.where(pos0 > 0, halo_ref[...], 0.0)
    ext = jnp.concatenate([halo, pu_ref[...]], axis=0)
    pos = pos0 + lax.broadcasted_iota(I32, (tm, grp), 0)
    out = x_ref[...] + _dot(a_ref[...], wo_ref[0:a_w, :])
    run = ext
    width = 1
    for g, w in enumerate(POOL_WINDOWS):
        while width < w:
            run = run + pltpu.roll(run, width, 0)
            width *= 2
        sl = slice(g * grp, (g + 1) * grp)
        wsum = run[halo_rows:, sl]
        cnt = jnp.minimum(pos + 1, w).astype(F32)
        d = wsum / cnt - pu_ref[:, sl]
        pg = _dot(d.astype(MXU_DTYPE), pw_ref[g]) * ps_ref[:, sl]
        out = out + _dot(pg.astype(MXU_DTYPE), wo_ref[a_w + g * grp:a_w + (g + 1) * grp, :])

    m_w = qm_ref.shape[1]
    for h in range(M_HEADS):
        sl = slice(h * HEAD_DIM, (h + 1) * HEAD_DIM)
        s = _dot_nt(qm_ref[:, sl], kvm_ref[:, sl])
        p = jnp.exp(s - jnp.max(s, axis=1, keepdims=True))
        o = _dot(p.astype(MXU_DTYPE), kvm_ref[:, m_w + h * HEAD_DIM:m_w + (h + 1) * HEAD_DIM])
        o = o / jnp.sum(p, axis=1, keepdims=True)
        r0 = a_w + p_w + h * HEAD_DIM
        out = out + _dot(o.astype(MXU_DTYPE), wo_ref[r0:r0 + HEAD_DIM, :])

    h_ref[...] = out
    hn_ref[...] = _rms(out, fg_ref[...]).astype(hn_ref.dtype)


def _top16_rows(s, row):
    rank = jnp.full(s.shape, float(PEER_TOPK), F32)
    big = float(s.shape[0])
    vals = []
    for k in range(PEER_TOPK):
        m = jnp.max(s, axis=0, keepdims=True)
        first = jnp.min(jnp.where(s == m, row, big), axis=0, keepdims=True)
        hit = row == first
        rank = jnp.where(hit, float(k), rank)
        s = jnp.where(hit, -jnp.inf, s)
        vals.append(m)
    return jnp.concatenate(vals, axis=0), rank


def _route_kernel(hn_ref, wq_ref, k1_ref, k2_ref, r2_ref, e2_ref, n_ref, c_ref, *, tm):
    q = _dot(hn_ref[...], wq_ref[...]).astype(MXU_DTYPE)
    row = lax.broadcasted_iota(I32, (N_KEYS, tm), 0).astype(F32)
    sub = lax.broadcasted_iota(I32, (8, tm), 0).astype(F32)
    half = N_KEYS
    for h in range(PEER_HEADS):
        s1 = _dot_nt(k1_ref[...], q[:, h * 2 * half:h * 2 * half + half])
        s2 = _dot_nt(k2_ref[...], q[:, h * 2 * half + half:(h + 1) * 2 * half])
        v1, rank1 = _top16_rows(s1, row)
        v2, rank2 = _top16_rows(s2, row)

        blocks, flats = [], []
        for half_a in range(2):
            blocks.append(v1[half_a * 8:(half_a + 1) * 8] + v2[0:1])
            flats.append((sub + half_a * 8) * PEER_TOPK)
        for b in range(1, 8):
            blk = v1[0:8] + v2[b:b + 1]
            blocks.append(jnp.where(sub < PEER_TOPK // (b + 1), blk, -jnp.inf))
            flats.append(sub * PEER_TOPK + b)
        blocks.append(v1[0:1] + v2[8:16])
        flats.append(sub + 8)
        cand = jnp.concatenate(blocks, axis=0)
        flat = jnp.concatenate(flats, axis=0)
        cand0 = cand
        sel = jnp.zeros(cand.shape, jnp.bool_)
        for _ in range(PEER_TOPK):
            m = jnp.max(cand, axis=0, keepdims=True)
            first = jnp.min(jnp.where(cand == m, flat, float(PEER_TOPK * PEER_TOPK)), axis=0, keepdims=True)
            hit = flat == first
            sel = jnp.logical_or(sel, hit)
            cand = jnp.where(hit, -jnp.inf, cand)
        top = v1[0:1] + v2[0:1]
        z = jnp.sum(jnp.where(sel, jnp.exp(cand0 - top), 0.0), axis=0, keepdims=True)
        self32 = sel.astype(F32)
        nb_lo = self32[0:8]
        for blk in range(2, 9):
            nb_lo = nb_lo + self32[blk * 8:(blk + 1) * 8]
        tail = jnp.sum(self32[72:80], axis=0, keepdims=True)
        nb_lo = nb_lo + jnp.where(sub == 0, tail, 0.0)
        nb = jnp.concatenate([nb_lo, self32[8:16]], axis=0)

        n_dense = jnp.zeros((N_KEYS, tm), F32)
        for a in range(PEER_TOPK):
            n_dense = jnp.where(rank1 == float(a), nb[a:a + 1], n_dense)
        r2_ref[h] = rank2
        e2_ref[h] = jnp.exp(s2 - v2[0:1])
        n_ref[h] = n_dense
        c_ref[h] = jnp.exp(s1 - v1[0:1]) / z


def _gelu(x):
    return 0.5 * x * (1.0 + lax.erf(x * (2.0 ** -0.5)))


def _peer_kernel(hn_ref, u_ref, vt_ref, r2_ref, e2_ref, n_ref, c_ref, h_ref, o_ref, acc_ref, at_ref, *, tb, eb):
    ei = pl.program_id(1)

    @pl.when(ei == 0)
    def _():
        acc_ref[...] = jnp.zeros(acc_ref.shape, F32)

    ht = _dot_nt(u_ref[...], hn_ref[...])
    ipb = eb // N_KEYS
    sub = 8
    base = pl.multiple_of((ei * ipb) // sub * sub, sub)
    off = (ei * ipb) % sub
    for il in range(ipb):
        for tcol in range(tb // LANES):
            cs = slice(tcol * LANES, (tcol + 1) * LANES)
            gate = jnp.zeros((N_KEYS, LANES), F32)
            for h in range(PEER_HEADS):
                n8 = n_ref[h, pl.ds(base, sub), cs]
                c8 = c_ref[h, pl.ds(base, sub), cs]
                nrow = n8[il:il + 1]
                crow = c8[il:il + 1]
                for o in range(ipb, sub, ipb):
                    nrow = jnp.where(off == o, n8[o + il:o + il + 1], nrow)
                    crow = jnp.where(off == o, c8[o + il:o + il + 1], crow)
                gate = gate + jnp.where(r2_ref[h, :, cs] < nrow, e2_ref[h, :, cs], 0.0) * crow
            a = gate * _gelu(ht[il * N_KEYS:(il + 1) * N_KEYS, cs])
            at_ref[il * N_KEYS:(il + 1) * N_KEYS, cs] = a.astype(at_ref.dtype)
    acc_ref[...] += _dot(vt_ref[...], at_ref[...])

    @pl.when(ei == pl.num_programs(1) - 1)
    def _():
        o_ref[...] = h_ref[...] + acc_ref[...].T


def _rope_tables(seq):
    pos = jnp.arange(seq, dtype=F32)[:, None]

    def table(half):
        inv = ROPE_THETA ** (-jnp.arange(half, dtype=F32) / half)
        ang = pos * inv[None, :]
        return jnp.cos(ang), jnp.sin(ang)

    c64, s64 = table(HEAD_DIM // 2)
    c32, s32 = table(IDX_DIM // 2)
    cos_a = jnp.concatenate([c64, c64], axis=1)
    sin_a = jnp.concatenate([-s64, s64], axis=1)
    cos_i = jnp.concatenate([c32, c32, c32, c32], axis=1)
    sin_i = jnp.concatenate([-s32, s32, -s32, s32], axis=1)
    return cos_a, sin_a, cos_i, sin_i


def _attn_steps(seq, tq, tk):
    qts, kts = [], []
    for qt in range(seq // tq):
        for kt in range(((qt + 1) * tq - 1) // tk + 1):
            qts.append(qt)
            kts.append(kt)
    return np.asarray(qts, np.int32), np.asarray(kts, np.int32)


def _full(shape):
    return pl.BlockSpec(shape, lambda *_: (0,) * len(shape))


def _layer(x, mem, attn_g, w_in, q_g, k_g, pool_w, pool_s, mem_g, w_mem_kv, mq_g, mk_g, w_out,
           ffn_g, peer_wq, sk1, sk2, peer_u, peer_v):
    b, seq, d = x.shape
    t = b * seq
    mem_len = mem.shape[1]
    a_w = A_HEADS * HEAD_DIM
    i_w = IDX_HEADS * IDX_DIM
    p_w = pool_w.shape[0] * pool_w.shape[1]
    m_w = M_HEADS * HEAD_DIM
    top_k = min(TOP_K_MAX, seq // 4)
    md = MXU_DTYPE

    x2 = x.reshape(t, d)
    row = lambda v: v.reshape(1, -1).astype(F32)
    cos_a, sin_a, cos_i, sin_i = _rope_tables(seq)

    o = 0
    w_qk = w_in[:, o:o + 2 * a_w].astype(md); o += 2 * a_w
    w_vt = w_in[:, o:o + a_w].T.astype(md); o += a_w
    w_qi = w_in[:, o:o + i_w].astype(md); o += i_w
    w_small = jnp.pad(w_in[:, o:o + IDX_DIM + IDX_HEADS], ((0, 0), (0, LANES - IDX_DIM - IDX_HEADS))).astype(md)
    o += IDX_DIM + IDX_HEADS
    w_pu = w_in[:, o:o + p_w].astype(md); o += p_w
    w_qm = w_in[:, o:o + m_w].astype(md)

    tm = min(1024, seq)
    n_rt = t // tm
    pos_blk = lambda i: i % (seq // tm)
    hg = jnp.stack([row(q_g) * (HEAD_DIM ** -0.5 * np.log2(np.e)), row(k_g)])
    qk = pl.pallas_call(
        _qk_kernel,
        grid=(n_rt, 2),
        in_specs=[pl.BlockSpec((tm, d), lambda i, j: (i, 0)),
                  pl.BlockSpec((1, d), lambda i, j: (0, 0)),
                  pl.BlockSpec((d, a_w), lambda i, j: (0, j)),
                  pl.BlockSpec((1, 1, HEAD_DIM), lambda i, j: (j, 0, 0)),
                  pl.BlockSpec((tm, HEAD_DIM), lambda i, j: (pos_blk(i), 0)),
                  pl.BlockSpec((tm, HEAD_DIM), lambda i, j: (pos_blk(i), 0))],
        out_specs=pl.BlockSpec((tm, a_w), lambda i, j: (i, j)),
        out_shape=jax.ShapeDtypeStruct((t, 2 * a_w), md),
        scratch_shapes=[pltpu.VMEM((tm, d), md)],
        compiler_params=_cparams("parallel", "arbitrary"),
        name="qk_proj",
    )(x2, row(attn_g), w_qk, hg, cos_a, sin_a)

    tmi = min(512, seq)
    vt = pl.pallas_call(
        _vt_kernel,
        grid=(t // tmi,),
        in_specs=[pl.BlockSpec((tmi, d), lambda i: (i, 0)), _full((1, d)), _full((a_w, d))],
        out_specs=pl.BlockSpec((a_w, tmi), lambda i: (0, i)),
        out_shape=jax.ShapeDtypeStruct((a_w, t), md),
        compiler_params=_cparams("parallel"),
        name="vt_proj",
    )(x2, row(attn_g), w_vt)

    expand = np.zeros((LANES, i_w), np.float32)
    for h in range(IDX_HEADS):
        expand[IDX_DIM + h, h * IDX_DIM:(h + 1) * IDX_DIM] = 1.0
    qi, kd, wt = pl.pallas_call(
        _idx_kernel,
        grid=(t // tmi,),
        in_specs=[pl.BlockSpec((tmi, d), lambda i: (i, 0)),
                  _full((1, d)), _full((d, i_w)), _full((d, LANES)), _full((LANES, i_w)),
                  pl.BlockSpec((tmi, LANES), lambda i: (i % (seq // tmi), 0)),
                  pl.BlockSpec((tmi, LANES), lambda i: (i % (seq // tmi), 0))],
        out_specs=[pl.BlockSpec((tmi, i_w), lambda i: (i, 0)),
                   pl.BlockSpec((tmi, LANES), lambda i: (i, 0)),
                   pl.BlockSpec((LANES, tmi), lambda i: (0, i))],
        out_shape=[jax.ShapeDtypeStruct((t, i_w), md),
                   jax.ShapeDtypeStruct((t, LANES), md),
                   jax.ShapeDtypeStruct((LANES, t), F32)],
        compiler_params=_cparams("parallel"),
        name="idx_proj",
    )(x2, row(attn_g), w_qi, w_small, jnp.asarray(expand, md), cos_i, sin_i)

    pu = pl.pallas_call(
        _plain_kernel,
        grid=(t // tmi,),
        in_specs=[pl.BlockSpec((tmi, d), lambda i: (i, 0)), _full((1, d)), _full((d, p_w))],
        out_specs=pl.BlockSpec((tmi, p_w), lambda i: (i, 0)),
        out_shape=jax.ShapeDtypeStruct((t, p_w), F32),
        compiler_params=_cparams("parallel"),
        name="pool_proj",
    )(x2, row(attn_g), w_pu)

    qm = pl.pallas_call(
        functools.partial(_headnorm_kernel, heads=M_HEADS),
        grid=(t // tmi,),
        in_specs=[pl.BlockSpec((tmi, d), lambda i: (i, 0)), _full((1, d)), _full((d, m_w)), _full((1, HEAD_DIM))],
        out_specs=pl.BlockSpec((tmi, m_w), lambda i: (i, 0)),
        out_shape=jax.ShapeDtypeStruct((t, m_w), md),
        compiler_params=_cparams("parallel"),
        name="memq_proj",
    )(x2, row(attn_g), w_qm, row(mq_g) * (HEAD_DIM ** -0.5))

    tmem = b * mem_len
    kvm = pl.pallas_call(
        _memkv_kernel,
        grid=(2,),
        in_specs=[_full((tmem, d)), _full((1, d)),
                  pl.BlockSpec((d, m_w), lambda j: (0, j)), _full((1, HEAD_DIM))],
        out_specs=pl.BlockSpec((tmem, m_w), lambda j: (0, j)),
        out_shape=jax.ShapeDtypeStruct((tmem, 2 * m_w), md),
        compiler_params=_cparams("arbitrary"),
        name="memkv_proj",
    )(mem.reshape(tmem, d), row(mem_g), w_mem_kv.astype(md), row(mk_g))

    tq, tks = 256, 512
    n_qt = seq // tq
    mask = pl.pallas_call(
        functools.partial(_select_kernel, tq=tq, tk=tks, seq=seq, top_k=top_k),
        grid=(b, n_qt),
        in_specs=[pl.BlockSpec((None, tq, i_w), lambda bi, qt: (bi, qt, 0)),
                  pl.BlockSpec((LANES, tq), lambda bi, qt: (0, bi * n_qt + qt)),
                  pl.BlockSpec((None, seq, LANES), lambda bi, qt: (bi, 0, 0))],
        out_specs=pl.BlockSpec((None, seq, tq), lambda bi, qt: (bi, 0, qt)),
        out_shape=jax.ShapeDtypeStruct((b, seq, seq), jnp.int8),
        scratch_shapes=[pltpu.VMEM((IDX_HEADS * tq, LANES), md),
                        pltpu.VMEM((IDX_HEADS, tq), F32),
                        pltpu.VMEM((IDX_HEADS, tq), F32),
                        pltpu.VMEM((seq, tq), I32)],
        compiler_params=_cparams("parallel", "arbitrary"),
        name="dsa_select",
    )(qi.reshape(b, seq, i_w), wt, kd.reshape(b, seq, LANES))

    tqa, tka = min(512, seq), min(1024, seq)
    n_kta = seq // tka
    qts, kts = _attn_steps(seq, tqa, tka)
    qk3 = qk.reshape(b, seq, 2 * a_w)
    a_out = pl.pallas_call(
        functools.partial(_attn_kernel, tq=tqa, tk=tka),
        grid_spec=pltpu.PrefetchScalarGridSpec(
            num_scalar_prefetch=2,
            grid=(b, len(qts)),
            in_specs=[pl.BlockSpec((None, tqa, a_w), lambda bi, s, qt, kt: (bi, qt[s], 0)),
                      pl.BlockSpec((None, tka, a_w), lambda bi, s, qt, kt: (bi, kt[s], 1)),
                      pl.BlockSpec((a_w, tka), lambda bi, s, qt, kt: (0, bi * n_kta + kt[s])),
                      pl.BlockSpec((None, tka, tqa), lambda bi, s, qt, kt: (bi, kt[s], qt[s]))],
            out_specs=pl.BlockSpec((None, tqa, a_w), lambda bi, s, qt, kt: (bi, qt[s], 0)),
            scratch_shapes=[pltpu.VMEM((A_HEADS, tqa), F32),
                            pltpu.VMEM((A_HEADS, tqa), F32),
                            pltpu.VMEM((a_w, tqa), F32),
                            pltpu.VMEM((2, tka, tqa), F32)]),
        out_shape=jax.ShapeDtypeStruct((b, seq, a_w), md),
        compiler_params=_cparams("parallel", "arbitrary"),
        name="dsa_attn",
    )(jnp.asarray(qts), jnp.asarray(kts), qk3, qk3, vt, mask)

    tmx = min(512, seq)
    halo_rows = POOL_WINDOWS[-1]
    hsub = tmx // halo_rows
    h_res, hn = pl.pallas_call(
        functools.partial(_mix_kernel, tm=tmx, seq=seq),
        grid=(t // tmx,),
        in_specs=[pl.BlockSpec((tmx, d), lambda i: (i, 0)),
                  pl.BlockSpec((tmx, a_w), lambda i: (i, 0)),
                  pl.BlockSpec((tmx, p_w), lambda i: (i, 0)),
                  pl.BlockSpec((halo_rows, p_w), lambda i: (jnp.maximum(i * hsub - 1, 0), 0)),
                  pl.BlockSpec((tmx, m_w), lambda i: (i, 0)),
                  pl.BlockSpec((mem_len, 2 * m_w), lambda i: (i // (seq // tmx), 0)),
                  _full(pool_w.shape), _full((1, p_w)), _full((d, d)), _full((1, d))],
        out_specs=[pl.BlockSpec((tmx, d), lambda i: (i, 0)),
                   pl.BlockSpec((tmx, d), lambda i: (i, 0))],
        out_shape=[jax.ShapeDtypeStruct((t, d), F32), jax.ShapeDtypeStruct((t, d), md)],
        compiler_params=_cparams("parallel"),
        name="mix_out",
    )(x2, a_out.reshape(t, a_w), pu, pu, qm, kvm, pool_w.astype(md), row(pool_s), w_out.astype(md), row(ffn_g))

    tmr = 256
    rshape = jax.ShapeDtypeStruct((PEER_HEADS, N_KEYS, t), F32)
    rspec = pl.BlockSpec((PEER_HEADS, N_KEYS, tmr), lambda i: (0, 0, i))
    r2, e2, nsel, csel = pl.pallas_call(
        functools.partial(_route_kernel, tm=tmr),
        grid=(t // tmr,),
        in_specs=[pl.BlockSpec((tmr, d), lambda i: (i, 0)),
                  _full(peer_wq.shape), _full(sk1.shape), _full(sk2.shape)],
        out_specs=[rspec, rspec, rspec, rspec],
        out_shape=[rshape, rshape, rshape, rshape],
        compiler_params=_cparams("parallel"),
        name="peer_route",
    )(hn, peer_wq.astype(md), sk1.astype(md), sk2.astype(md))

    tb, eb = 512, 512
    n_exp = peer_u.shape[0]
    gspec = pl.BlockSpec((PEER_HEADS, N_KEYS, tb), lambda ti, ei: (0, 0, ti))
    out = pl.pallas_call(
        functools.partial(_peer_kernel, tb=tb, eb=eb),
        grid=(t // tb, n_exp // eb),
        in_specs=[pl.BlockSpec((tb, d), lambda ti, ei: (ti, 0)),
                  pl.BlockSpec((eb, d), lambda ti, ei: (ei, 0)),
                  pl.BlockSpec((d, eb), lambda ti, ei: (0, ei)),
                  gspec, gspec, gspec, gspec,
                  pl.BlockSpec((tb, d), lambda ti, ei: (ti, 0))],
        out_specs=pl.BlockSpec((tb, d), lambda ti, ei: (ti, 0)),
        out_shape=jax.ShapeDtypeStruct((t, d), F32),
        scratch_shapes=[pltpu.VMEM((d, tb), F32), pltpu.VMEM((eb, tb), md)],
        compiler_params=_cparams("parallel", "arbitrary"),
        name="peer_experts",
    )(hn, peer_u.astype(md), peer_v.T.astype(md), r2, e2, nsel, csel, h_res)
    return out.reshape(b, seq, d)


def kernel(x, mem, attn_norm_gain, w_in, q_norm_gain, k_norm_gain, pool_w, pool_scale, mem_norm_gain,
           w_mem_kv, mq_norm_gain, mk_norm_gain, w_out, ffn_norm_gain, peer_w_q, peer_sub_keys_1,
           peer_sub_keys_2, peer_u, peer_v):
    for l in range(attn_norm_gain.shape[0]):
        x = _layer(x, mem, attn_norm_gain[l], w_in[l], q_norm_gain[l], k_norm_gain[l], pool_w[l],
                   pool_scale[l], mem_norm_gain[l], w_mem_kv[l], mq_norm_gain[l], mk_norm_gain[l],
                   w_out[l], ffn_norm_gain[l], peer_w_q[l], peer_sub_keys_1[l], peer_sub_keys_2[l],
                   peer_u[l], peer_v[l])
    return x
```

```python
import functools

import numpy as np
import jax
import jax.numpy as jnp
from jax import lax
from jax.experimental import pallas as pl
from jax.experimental.pallas import tpu as pltpu

F32 = jnp.float32
I32 = jnp.int32
MXU_DTYPE = jnp.bfloat16

LANES = 128
VMEM_LIMIT = 56 * 1024 * 1024

EPS = 1e-6
ROPE_THETA = 10000.0
CHUNK = 64
A_HEADS = 8
HEAD_DIM = 128
IDX_HEADS = 16
IDX_DIM = 64
TOP_K_MAX = 256
POOL_WINDOWS = (2, 4, 8, 16)
M_HEADS = 4
PEER_HEADS = 8
N_KEYS = 128
PEER_TOPK = 16
INT_MIN = -(2 ** 31)
NEG_BIG = -1e30


def _cparams(*sem):
    return pltpu.CompilerParams(dimension_semantics=sem, vmem_limit_bytes=VMEM_LIMIT)


def _rms(x, g):
    ms = jnp.mean(x * x, axis=-1, keepdims=True)
    return x * lax.rsqrt(ms + EPS) * g


def _dot(a, b):
    return jnp.dot(a, b, preferred_element_type=F32)


def _dot_nt(a, b):
    return lax.dot_general(a, b, (((1,), (1,)), ((), ())), preferred_element_type=F32)


def _lane_iota(shape):
    return lax.broadcasted_iota(I32, shape, len(shape) - 1)


def _qk_kernel(x_ref, g_ref, w_ref, hg_ref, cos_ref, sin_ref, o_ref, xn_ref):
    j = pl.program_id(1)

    @pl.when(j == 0)
    def _():
        xn_ref[...] = _rms(x_ref[...], g_ref[...]).astype(xn_ref.dtype)

    z = _dot(xn_ref[...], w_ref[...])

    cosf = cos_ref[...]
    sinf = sin_ref[...]
    hg = hg_ref[0]
    for h in range(A_HEADS):
        sl = slice(h * HEAD_DIM, (h + 1) * HEAD_DIM)
        n = _rms(z[:, sl], hg)
        o_ref[:, sl] = (n * cosf + pltpu.roll(n, HEAD_DIM // 2, 1) * sinf).astype(o_ref.dtype)


def _rope64(xb, cos4, sin4, first_half):
    rot = jnp.where(first_half, pltpu.roll(xb, LANES - IDX_DIM // 2, 1), pltpu.roll(xb, IDX_DIM // 2, 1))
    return xb * cos4 + rot * sin4


def _split3(v):
    hi = v.astype(MXU_DTYPE)
    r1 = v - hi.astype(F32)
    mid = r1.astype(MXU_DTYPE)
    lo = (r1 - mid.astype(F32)).astype(MXU_DTYPE)
    return hi, mid, lo


def _aux_proj_kernel(x_ref, g_ref, wvt_ref, wq_ref, ws_ref, e_ref, cos_ref, sin_ref, wpu_ref, wqm_ref, mqg_ref,
                     vt_ref, qi_ref, kd_ref, wt_ref, pu_ref, qm_ref):
    xn = _rms(x_ref[...], g_ref[...]).astype(MXU_DTYPE)
    vt_ref[...] = _dot_nt(wvt_ref[...], xn).astype(vt_ref.dtype)
    pu_ref[...] = _dot(xn, wpu_ref[...]).astype(pu_ref.dtype)
    zm = _dot(xn, wqm_ref[...])
    mqg = mqg_ref[...]
    for h in range(M_HEADS):
        sl = slice(h * HEAD_DIM, (h + 1) * HEAD_DIM)
        qm_ref[:, sl] = _rms(zm[:, sl], mqg).astype(qm_ref.dtype)

    zq = _dot(xn, wq_ref[...])
    zs = _dot(xn, ws_ref[...])
    wt_ref[...] = zs.T
    hi, mid, lo = _split3(zs)
    e = e_ref[...]
    wexp = _dot(hi, e) + _dot(mid, e) + _dot(lo, e)
    cos4 = cos_ref[...]
    sin4 = sin_ref[...]
    lane = _lane_iota(zs.shape)
    first_half = (lane & (IDX_DIM // 2)) == 0
    scale = (IDX_DIM ** -0.5) * (IDX_HEADS ** -0.5)
    for c in range(zq.shape[1] // LANES):
        sl = slice(c * LANES, (c + 1) * LANES)
        y = _rope64(zq[:, sl], cos4, sin4, first_half)
        qi_ref[:, sl] = (y * (wexp[:, sl] * scale)).astype(qi_ref.dtype)
    yk = _rope64(zs, cos4, sin4, first_half)
    kd_ref[...] = jnp.where(lane < IDX_DIM, yk, pltpu.roll(yk, IDX_DIM, 1)).astype(kd_ref.dtype)


def _memkv_kernel(x_ref, g_ref, w_ref, hg_ref, o_ref):
    j = pl.program_id(0)
    xn = _rms(x_ref[...], g_ref[...]).astype(MXU_DTYPE)
    z = _dot(xn, w_ref[...])

    @pl.when(j == 0)
    def _():
        hg = hg_ref[...]
        for h in range(M_HEADS):
            sl = slice(h * HEAD_DIM, (h + 1) * HEAD_DIM)
            o_ref[:, sl] = _rms(z[:, sl], hg).astype(o_ref.dtype)

    @pl.when(j == 1)
    def _():
        o_ref[...] = z.astype(o_ref.dtype)


def _select_kernel(qi_ref, wt_ref, kd_ref, mask_ref, lhs_ref, lo_ref, hi_ref, key_ref, *, tq, tk, seq, top_k):
    qt = pl.program_id(1)
    t0 = qt * tq
    n_kt = (t0 + tq + tk - 1) // tk
    hgrp = 4
    rb = 128
    n_rb = tk // rb
    cr = 32

    lane = _lane_iota((tq, LANES))
    for h in range(IDX_HEADS):
        pair = qi_ref[:, (h // 2) * LANES:(h // 2 + 1) * LANES]
        own = (lane < IDX_DIM) if h % 2 == 0 else (lane >= IDX_DIM)
        lhs_ref[h * tq:(h + 1) * tq, :] = jnp.where(own, pair, jnp.zeros_like(pair))
    w = wt_ref[IDX_DIM:IDX_DIM + IDX_HEADS, :]
    lo_ref[...] = jnp.where(w > 0, 0.0, -jnp.inf).astype(F32)
    hi_ref[...] = jnp.where(w > 0, jnp.inf, 0.0).astype(F32)

    q_lim = ((t0 + _lane_iota((1, tq))) // CHUNK + 1) * CHUNK

    def rows(kt, r):
        return pl.ds(pl.multiple_of(kt * tk + r * rb, rb), rb)

    def score_tile(kt, carry):
        k0 = pl.multiple_of(kt * tk, tk)
        kd = kd_ref[pl.ds(k0, tk), :]
        accs = [jnp.zeros((rb, tq), F32) for _ in range(n_rb)]
        for g in range(IDX_HEADS // hgrp):
            raw = _dot_nt(kd, lhs_ref[g * hgrp * tq:(g + 1) * hgrp * tq, :])
            for hh in range(hgrp):
                h = g * hgrp + hh
                lo = lo_ref[h:h + 1, :]
                hi = hi_ref[h:h + 1, :]
                for r in range(n_rb):
                    blk = raw[r * rb:(r + 1) * rb, hh * tq:(hh + 1) * tq]
                    accs[r] = accs[r] + jnp.minimum(jnp.maximum(blk, lo), hi)
        for r in range(n_rb):
            bits = pltpu.bitcast(accs[r], I32)
            key = bits ^ ((bits >> 31) & 0x7FFFFFFF)
            key_ref[rows(kt, r), :] = jnp.where(k0 + r * rb + lax.broadcasted_iota(I32, (rb, tq), 0) < q_lim, key, INT_MIN)
        return carry

    lax.fori_loop(0, n_kt, score_tile, 0)

    def srow(kt, r):
        return kt * tk + r * rb + lax.broadcasted_iota(I32, (rb, tq), 0)

    def count_keys(ind_fn):
        def tile(kt, cnt):
            for r in range(n_rb):
                ind = ind_fn(key_ref[rows(kt, r), :], kt, r)
                for c in range(rb // cr):
                    cnt = cnt + ind[c * cr:(c + 1) * cr]
            return cnt

        def pair(p, cnt):
            return tile(2 * p + 1, tile(2 * p, cnt))

        cnt = lax.fori_loop(0, n_kt // 2, pair, jnp.zeros((cr, tq), I32))
        cnt = lax.fori_loop(n_kt // 2 * 2, n_kt, tile, cnt)
        return jnp.sum(cnt, axis=0, keepdims=True)

    def bit_step(i, carry):
        prefix, n_ge = carry
        cand_u = prefix | lax.shift_left(jnp.int32(1), jnp.int32(31) - i)
        cand = cand_u ^ INT_MIN
        total = count_keys(lambda key, kt, r: jnp.where(key >= cand, 1, 0))
        keep = total >= top_k
        return jnp.where(keep, cand_u, prefix), jnp.where(keep, total, n_ge)

    group = 4
    few = q_lim <= top_k

    def bit_group(state):
        g, prefix, n_ge, _ = state
        for j in range(group):
            prefix, n_ge = bit_step(g * group + j, (prefix, n_ge))
        open_ = jnp.logical_and(n_ge != top_k, jnp.logical_not(few))
        return g + 1, prefix, n_ge, jnp.max(open_.astype(I32))

    zero_row = jnp.zeros((1, tq), I32)
    state = (jnp.int32(0), zero_row, zero_row, jnp.int32(1))
    _, prefix, n_ge, _ = lax.while_loop(lambda st: jnp.logical_and(st[0] < 32 // group, st[3] > 0), bit_group, state)
    tau = prefix ^ INT_MIN

    def write_mask(sel_fn):
        def write_tile(kt, carry):
            for r in range(n_rb):
                key = key_ref[rows(kt, r), :]
                sel = jnp.logical_and(sel_fn(key, kt, r), key > INT_MIN)
                mask_ref[rows(kt, r), :] = sel.astype(mask_ref.dtype)
            return carry

        lax.fori_loop(0, n_kt, write_tile, 0)

    tied = n_ge > top_k
    any_tied = jnp.max(tied.astype(I32))

    @pl.when(any_tied == 0)
    def _():
        write_mask(lambda key, kt, r: key >= tau)

    @pl.when(any_tied > 0)
    def _():
        n_gt = count_keys(lambda key, kt, r: jnp.where(key > tau, 1, 0))
        need = jnp.where(tied, top_k - n_gt, 2 ** 31 - 1)
        idx_bits = max(1, (seq - 1).bit_length())

        def idx_step(i, p):
            cand = p | lax.shift_left(jnp.int32(1), jnp.int32(idx_bits - 1) - i)
            before = count_keys(lambda key, kt, r: jnp.where(jnp.logical_and(key == tau, srow(kt, r) < cand), 1, 0))
            return jnp.where(before < need, cand, p)

        cut = lax.fori_loop(0, idx_bits, idx_step, zero_row)
        write_mask(lambda key, kt, r: jnp.logical_or(key > tau, jnp.logical_and(key == tau, srow(kt, r) <= cut)))

    def zero_tile(kt, carry):
        mask_ref[pl.ds(pl.multiple_of(kt * tk, tk), tk), :] = jnp.zeros((tk, tq), mask_ref.dtype)
        return carry

    lax.fori_loop(n_kt, seq // tk, zero_tile, 0)


def _attn_kernel(qt_tab, kt_tab, q_ref, k_ref, vt_ref, mask_ref, o_ref, m_ref, l_ref, acc_ref, s_ref, *, tq, tk):
    step = pl.program_id(1)
    qt = qt_tab[step]
    kt = kt_tab[step]
    last_kt = ((qt + 1) * tq - 1) // tk

    @pl.when(kt == 0)
    def _():
        m_ref[...] = jnp.full(m_ref.shape, NEG_BIG, F32)
        l_ref[...] = jnp.zeros(l_ref.shape, F32)
        acc_ref[...] = jnp.zeros(acc_ref.shape, F32)

    bias = jnp.where(mask_ref[...].astype(I32) != 0, 0.0, -jnp.inf).astype(F32)
    hs = lambda h: slice(h * HEAD_DIM, (h + 1) * HEAD_DIM)

    def logits(h):
        s_ref[h % 2] = _dot_nt(k_ref[:, hs(h)], q_ref[:, hs(h)])

    logits(0)
    for h in range(A_HEADS):
        if h + 1 < A_HEADS:
            logits(h + 1)
        s = s_ref[h % 2] + bias
        m_prev = m_ref[h:h + 1, :]
        m_new = jnp.maximum(m_prev, jnp.max(s, axis=0, keepdims=True))
        alpha = jnp.exp2(m_prev - m_new)
        p = jnp.exp2(s - m_new)
        l_ref[h:h + 1, :] = alpha * l_ref[h:h + 1, :] + jnp.sum(p, axis=0, keepdims=True)
        acc_ref[hs(h), :] = alpha * acc_ref[hs(h), :] + _dot(vt_ref[hs(h), :], p.astype(MXU_DTYPE))
        m_ref[h:h + 1, :] = m_new

    @pl.when(kt == last_kt)
    def _():
        for h in range(A_HEADS):
            o_ref[:, hs(h)] = (acc_ref[hs(h), :] / l_ref[h:h + 1, :]).T.astype(o_ref.dtype)


def _mix_kernel(x_ref, a_ref, pu_ref, halo_ref, qm_ref, kvm_ref, pw_ref, ps_ref, wo_ref, fg_ref,
                h_ref, hn_ref, *, tm, seq):
    i = pl.program_id(0)
    tiles_per_seq = seq // tm
    pos0 = (i % tiles_per_seq) * tm
    halo_rows = halo_ref.shape[0]
    a_w = a_ref.shape[1]
    p_w = pu_ref.shape[1]
    grp = p_w // len(POOL_WINDOWS)

    halo = jnp.where(pos0 > 0, halo_ref[...], 0.0)
    ext = jnp.concatenate([halo, pu_ref[...]], axis=0)
    pos = pos0 + lax.broadcasted_iota(I32, (tm, grp), 0)
    out = x_ref[...] + _dot(a_ref[...], wo_ref[0:a_w, :])
    run = ext
    width = 1
    for g, w in enumerate(POOL_WINDOWS):
        while width < w:
            run = run + pltpu.roll(run, width, 0)
            width *= 2
        sl = slice(g * grp, (g + 1) * grp)
        wsum = run[halo_rows:, sl]
        cnt = jnp.minimum(pos + 1, w).astype(F32)
        d = wsum / cnt - pu_ref[:, sl]
        pg = _dot(d.astype(MXU_DTYPE), pw_ref[g]) * ps_ref[:, sl]
        out = out + _dot(pg.astype(MXU_DTYPE), wo_ref[a_w + g * grp:a_w + (g + 1) * grp, :])

    m_w = qm_ref.shape[1]
    for h in range(M_HEADS):
        sl = slice(h * HEAD_DIM, (h + 1) * HEAD_DIM)
        s = _dot_nt(qm_ref[:, sl], kvm_ref[:, sl])
        p = jnp.exp(s - jnp.max(s, axis=1, keepdims=True))
        o = _dot(p.astype(MXU_DTYPE), kvm_ref[:, m_w + h * HEAD_DIM:m_w + (h + 1) * HEAD_DIM])
        o = o / jnp.sum(p, axis=1, keepdims=True)
        r0 = a_w + p_w + h * HEAD_DIM
        out = out + _dot(o.astype(MXU_DTYPE), wo_ref[r0:r0 + HEAD_DIM, :])

    h_ref[...] = out
    hn_ref[...] = _rms(out, fg_ref[...]).astype(hn_ref.dtype)


def _top16_rows(s, row):
    rank = jnp.full(s.shape, float(PEER_TOPK), F32)
    big = float(s.shape[0])
    vals = []
    for k in range(PEER_TOPK):
        m = jnp.max(s, axis=0, keepdims=True)
        first = jnp.min(jnp.where(s == m, row, big), axis=0, keepdims=True)
        hit = row == first
        rank = jnp.where(hit, float(k), rank)
        s = jnp.where(hit, -jnp.inf, s)
        vals.append(m)
    return jnp.concatenate(vals, axis=0), rank


def _route_kernel(hn_ref, wq_ref, k1_ref, k2_ref, r2_ref, e2_ref, n_ref, c_ref, *, tm):
    q = _dot(hn_ref[...], wq_ref[...]).astype(MXU_DTYPE)
    row = lax.broadcasted_iota(I32, (N_KEYS, tm), 0).astype(F32)
    sub = lax.broadcasted_iota(I32, (8, tm), 0).astype(F32)
    half = N_KEYS
    for h in range(PEER_HEADS):
        s1 = _dot_nt(k1_ref[...], q[:, h * 2 * half:h * 2 * half + half])
        s2 = _dot_nt(k2_ref[...], q[:, h * 2 * half + half:(h + 1) * 2 * half])
        v1, rank1 = _top16_rows(s1, row)
        v2, rank2 = _top16_rows(s2, row)

        blocks, flats = [], []
        for half_a in range(2):
            blocks.append(v1[half_a * 8:(half_a + 1) * 8] + v2[0:1])
            flats.append((sub + half_a * 8) * PEER_TOPK)
        for b in range(1, 8):
            blk = v1[0:8] + v2[b:b + 1]
            blocks.append(jnp.where(sub < PEER_TOPK // (b + 1), blk, -jnp.inf))
            flats.append(sub * PEER_TOPK + b)
        blocks.append(v1[0:1] + v2[8:16])
        flats.append(sub + 8)
        cand = jnp.concatenate(blocks, axis=0)
        flat = jnp.concatenate(flats, axis=0)
        cand0 = cand
        sel = jnp.zeros(cand.shape, jnp.bool_)
        for _ in range(PEER_TOPK):
            m = jnp.max(cand, axis=0, keepdims=True)
            first = jnp.min(jnp.where(cand == m, flat, float(PEER_TOPK * PEER_TOPK)), axis=0, keepdims=True)
            hit = flat == first
            sel = jnp.logical_or(sel, hit)
            cand = jnp.where(hit, -jnp.inf, cand)
        top = v1[0:1] + v2[0:1]
        z = jnp.sum(jnp.where(sel, jnp.exp(cand0 - top), 0.0), axis=0, keepdims=True)
        self32 = sel.astype(F32)
        nb_lo = self32[0:8]
        for blk in range(2, 9):
            nb_lo = nb_lo + self32[blk * 8:(blk + 1) * 8]
        tail = jnp.sum(self32[72:80], axis=0, keepdims=True)
        nb_lo = nb_lo + jnp.where(sub == 0, tail, 0.0)
        nb = jnp.concatenate([nb_lo, self32[8:16]], axis=0)

        n_dense = jnp.zeros((N_KEYS, tm), F32)
        for a in range(PEER_TOPK):
            n_dense = jnp.where(rank1 == float(a), nb[a:a + 1], n_dense)
        r2_ref[h] = rank2
        e2_ref[h] = jnp.exp(s2 - v2[0:1])
        n_ref[h] = n_dense
        c_ref[h] = jnp.exp(s1 - v1[0:1]) / z


def _gelu(x):
    return 0.5 * x * (1.0 + lax.erf(x * (2.0 ** -0.5)))


def _peer_kernel(hn_ref, u_ref, vt_ref, r2_ref, e2_ref, n_ref, c_ref, h_ref, o_ref, acc_ref, at_ref, *, tb, eb):
    ei = pl.program_id(1)

    @pl.when(ei == 0)
    def _():
        acc_ref[...] = jnp.zeros(acc_ref.shape, F32)

    ht = _dot_nt(u_ref[...], hn_ref[...])
    ipb = eb // N_KEYS
    sub = 8
    base = pl.multiple_of((ei * ipb) // sub * sub, sub)
    off = (ei * ipb) % sub
    for il in range(ipb):
        for tcol in range(tb // LANES):
            cs = slice(tcol * LANES, (tcol + 1) * LANES)
            gate = jnp.zeros((N_KEYS, LANES), F32)
            for h in range(PEER_HEADS):
                n8 = n_ref[h, pl.ds(base, sub), cs]
                c8 = c_ref[h, pl.ds(base, sub), cs]
                nrow = n8[il:il + 1]
                crow = c8[il:il + 1]
                for o in range(ipb, sub, ipb):
                    nrow = jnp.where(off == o, n8[o + il:o + il + 1], nrow)
                    crow = jnp.where(off == o, c8[o + il:o + il + 1], crow)
                gate = gate + jnp.where(r2_ref[h, :, cs] < nrow, e2_ref[h, :, cs], 0.0) * crow
            a = gate * _gelu(ht[il * N_KEYS:(il + 1) * N_KEYS, cs])
            at_ref[il * N_KEYS:(il + 1) * N_KEYS, cs] = a.astype(at_ref.dtype)
    acc_ref[...] += _dot(vt_ref[...], at_ref[...])

    @pl.when(ei == pl.num_programs(1) - 1)
    def _():
        o_ref[...] = h_ref[...] + acc_ref[...].T


def _rope_tables(seq):
    pos = jnp.arange(seq, dtype=F32)[:, None]

    def table(half):
        inv = ROPE_THETA ** (-jnp.arange(half, dtype=F32) / half)
        ang = pos * inv[None, :]
        return jnp.cos(ang), jnp.sin(ang)

    c64, s64 = table(HEAD_DIM // 2)
    c32, s32 = table(IDX_DIM // 2)
    cos_a = jnp.concatenate([c64, c64], axis=1)
    sin_a = jnp.concatenate([-s64, s64], axis=1)
    cos_i = jnp.concatenate([c32, c32, c32, c32], axis=1)
    sin_i = jnp.concatenate([-s32, s32, -s32, s32], axis=1)
    return cos_a, sin_a, cos_i, sin_i


def _attn_steps(seq, tq, tk):
    qts, kts = [], []
    for qt in range(seq // tq):
        for kt in range(((qt + 1) * tq - 1) // tk + 1):
            qts.append(qt)
            kts.append(kt)
    return np.asarray(qts, np.int32), np.asarray(kts, np.int32)


def _full(shape):
    return pl.BlockSpec(shape, lambda *_: (0,) * len(shape))


def _layer(x, mem, attn_g, w_in, q_g, k_g, pool_w, pool_s, mem_g, w_mem_kv, mq_g, mk_g, w_out,
           ffn_g, peer_wq, sk1, sk2, peer_u, peer_v):
    b, seq, d = x.shape
    t = b * seq
    mem_len = mem.shape[1]
    a_w = A_HEADS * HEAD_DIM
    i_w = IDX_HEADS * IDX_DIM
    p_w = pool_w.shape[0] * pool_w.shape[1]
    m_w = M_HEADS * HEAD_DIM
    top_k = min(TOP_K_MAX, seq // 4)
    md = MXU_DTYPE

    x2 = x.reshape(t, d)
    row = lambda v: v.reshape(1, -1).astype(F32)
    cos_a, sin_a, cos_i, sin_i = _rope_tables(seq)

    o = 0
    w_qk = w_in[:, o:o + 2 * a_w].astype(md); o += 2 * a_w
    w_vt = w_in[:, o:o + a_w].T.astype(md); o += a_w
    w_qi = w_in[:, o:o + i_w].astype(md); o += i_w
    w_small = jnp.pad(w_in[:, o:o + IDX_DIM + IDX_HEADS], ((0, 0), (0, LANES - IDX_DIM - IDX_HEADS))).astype(md)
    o += IDX_DIM + IDX_HEADS
    w_pu = w_in[:, o:o + p_w].astype(md); o += p_w
    w_qm = w_in[:, o:o + m_w].astype(md)

    tm = min(1024, seq)
    n_rt = t // tm
    pos_blk = lambda i: i % (seq // tm)
    hg = jnp.stack([row(q_g) * (HEAD_DIM ** -0.5 * np.log2(np.e)), row(k_g)])
    qk = pl.pallas_call(
        _qk_kernel,
        grid=(n_rt, 2),
        in_specs=[pl.BlockSpec((tm, d), lambda i, j: (i, 0)),
                  pl.BlockSpec((1, d), lambda i, j: (0, 0)),
                  pl.BlockSpec((d, a_w), lambda i, j: (0, j)),
                  pl.BlockSpec((1, 1, HEAD_DIM), lambda i, j: (j, 0, 0)),
                  pl.BlockSpec((tm, HEAD_DIM), lambda i, j: (pos_blk(i), 0)),
                  pl.BlockSpec((tm, HEAD_DIM), lambda i, j: (pos_blk(i), 0))],
        out_specs=pl.BlockSpec((tm, a_w), lambda i, j: (i, j)),
        out_shape=jax.ShapeDtypeStruct((t, 2 * a_w), md),
        scratch_shapes=[pltpu.VMEM((tm, d), md)],
        compiler_params=_cparams("parallel", "arbitrary"),
        name="qk_proj",
    )(x2, row(attn_g), w_qk, hg, cos_a, sin_a)

    tmi = min(512, seq)
    expand = np.zeros((LANES, i_w), np.float32)
    for h in range(IDX_HEADS):
        expand[IDX_DIM + h, h * IDX_DIM:(h + 1) * IDX_DIM] = 1.0
    tok = lambda w: pl.BlockSpec((tmi, w), lambda i: (i, 0))
    tok_t = lambda w: pl.BlockSpec((w, tmi), lambda i: (0, i))
    rope_spec = pl.BlockSpec((tmi, LANES), lambda i: (i % (seq // tmi), 0))
    vt, qi, kd, wt, pu, qm = pl.pallas_call(
        _aux_proj_kernel,
        grid=(t // tmi,),
        in_specs=[tok(d), _full((1, d)), _full((a_w, d)), _full((d, i_w)), _full((d, LANES)), _full((LANES, i_w)),
                  rope_spec, rope_spec, _full((d, p_w)), _full((d, m_w)), _full((1, HEAD_DIM))],
        out_specs=[tok_t(a_w), tok(i_w), tok(LANES), tok_t(LANES), tok(p_w), tok(m_w)],
        out_shape=[jax.ShapeDtypeStruct((a_w, t), md),
                   jax.ShapeDtypeStruct((t, i_w), md),
                   jax.ShapeDtypeStruct((t, LANES), md),
                   jax.ShapeDtypeStruct((LANES, t), F32),
                   jax.ShapeDtypeStruct((t, p_w), F32),
                   jax.ShapeDtypeStruct((t, m_w), md)],
        compiler_params=_cparams("parallel"),
        name="aux_proj",
    )(x2, row(attn_g), w_vt, w_qi, w_small, jnp.asarray(expand, md), cos_i, sin_i, w_pu, w_qm,
      row(mq_g) * (HEAD_DIM ** -0.5))

    tmem = b * mem_len
    kvm = pl.pallas_call(
        _memkv_kernel,
        grid=(2,),
        in_specs=[_full((tmem, d)), _full((1, d)),
                  pl.BlockSpec((d, m_w), lambda j: (0, j)), _full((1, HEAD_DIM))],
        out_specs=pl.BlockSpec((tmem, m_w), lambda j: (0, j)),
        out_shape=jax.ShapeDtypeStruct((tmem, 2 * m_w), md),
        compiler_params=_cparams("arbitrary"),
        name="memkv_proj",
    )(mem.reshape(tmem, d), row(mem_g), w_mem_kv.astype(md), row(mk_g))

    tq, tks = 256, 512
    n_qt = seq // tq
    mask = pl.pallas_call(
        functools.partial(_select_kernel, tq=tq, tk=tks, seq=seq, top_k=top_k),
        grid=(b, n_qt),
        in_specs=[pl.BlockSpec((None, tq, i_w), lambda bi, qt: (bi, qt, 0)),
                  pl.BlockSpec((LANES, tq), lambda bi, qt: (0, bi * n_qt + qt)),
                  pl.BlockSpec((None, seq, LANES), lambda bi, qt: (bi, 0, 0))],
        out_specs=pl.BlockSpec((None, seq, tq), lambda bi, qt: (bi, 0, qt)),
        out_shape=jax.ShapeDtypeStruct((b, seq, seq), jnp.int8),
        scratch_shapes=[pltpu.VMEM((IDX_HEADS * tq, LANES), md),
                        pltpu.VMEM((IDX_HEADS, tq), F32),
                        pltpu.VMEM((IDX_HEADS, tq), F32),
                        pltpu.VMEM((seq, tq), I32)],
        compiler_params=_cparams("parallel", "arbitrary"),
        name="dsa_select",
    )(qi.reshape(b, seq, i_w), wt, kd.reshape(b, seq, LANES))

    tqa, tka = min(512, seq), min(1024, seq)
    n_kta = seq // tka
    qts, kts = _attn_steps(seq, tqa, tka)
    qk3 = qk.reshape(b, seq, 2 * a_w)
    a_out = pl.pallas_call(
        functools.partial(_attn_kernel, tq=tqa, tk=tka),
        grid_spec=pltpu.PrefetchScalarGridSpec(
            num_scalar_prefetch=2,
            grid=(b, len(qts)),
            in_specs=[pl.BlockSpec((None, tqa, a_w), lambda bi, s, qt, kt: (bi, qt[s], 0)),
                      pl.BlockSpec((None, tka, a_w), lambda bi, s, qt, kt: (bi, kt[s], 1)),
                      pl.BlockSpec((a_w, tka), lambda bi, s, qt, kt: (0, bi * n_kta + kt[s])),
                      pl.BlockSpec((None, tka, tqa), lambda bi, s, qt, kt: (bi, kt[s], qt[s]))],
            out_specs=pl.BlockSpec((None, tqa, a_w), lambda bi, s, qt, kt: (bi, qt[s], 0)),
            scratch_shapes=[pltpu.VMEM((A_HEADS, tqa), F32),
                            pltpu.VMEM((A_HEADS, tqa), F32),
                            pltpu.VMEM((a_w, tqa), F32),
                            pltpu.VMEM((2, tka, tqa), F32)]),
        out_shape=jax.ShapeDtypeStruct((b, seq, a_w), md),
        compiler_params=_cparams("parallel", "arbitrary"),
        name="dsa_attn",
    )(jnp.asarray(qts), jnp.asarray(kts), qk3, qk3, vt, mask)

    tmx = min(512, seq)
    halo_rows = POOL_WINDOWS[-1]
    hsub = tmx // halo_rows
    h_res, hn = pl.pallas_call(
        functools.partial(_mix_kernel, tm=tmx, seq=seq),
        grid=(t // tmx,),
        in_specs=[pl.BlockSpec((tmx, d), lambda i: (i, 0)),
                  pl.BlockSpec((tmx, a_w), lambda i: (i, 0)),
                  pl.BlockSpec((tmx, p_w), lambda i: (i, 0)),
                  pl.BlockSpec((halo_rows, p_w), lambda i: (jnp.maximum(i * hsub - 1, 0), 0)),
                  pl.BlockSpec((tmx, m_w), lambda i: (i, 0)),
                  pl.BlockSpec((mem_len, 2 * m_w), lambda i: (i // (seq // tmx), 0)),
                  _full(pool_w.shape), _full((1, p_w)), _full((d, d)), _full((1, d))],
        out_specs=[pl.BlockSpec((tmx, d), lambda i: (i, 0)),
                   pl.BlockSpec((tmx, d), lambda i: (i, 0))],
        out_shape=[jax.ShapeDtypeStruct((t, d), F32), jax.ShapeDtypeStruct((t, d), md)],
        compiler_params=_cparams("parallel"),
        name="mix_out",
    )(x2, a_out.reshape(t, a_w), pu, pu, qm, kvm, pool_w.astype(md), row(pool_s), w_out.astype(md), row(ffn_g))

    tmr = 256
    rshape = jax.ShapeDtypeStruct((PEER_HEADS, N_KEYS, t), F32)
    rspec = pl.BlockSpec((PEER_HEADS, N_KEYS, tmr), lambda i: (0, 0, i))
    r2, e2, nsel, csel = pl.pallas_call(
        functools.partial(_route_kernel, tm=tmr),
        grid=(t // tmr,),
        in_specs=[pl.BlockSpec((tmr, d), lambda i: (i, 0)),
                  _full(peer_wq.shape), _full(sk1.shape), _full(sk2.shape)],
        out_specs=[rspec, rspec, rspec, rspec],
        out_shape=[rshape, rshape, rshape, rshape],
        compiler_params=_cparams("parallel"),
        name="peer_route",
    )(hn, peer_wq.astype(md), sk1.astype(md), sk2.astype(md))

    tb, eb = 512, 512
    n_exp = peer_u.shape[0]
    gspec = pl.BlockSpec((PEER_HEADS, N_KEYS, tb), lambda ti, ei: (0, 0, ti))
    out = pl.pallas_call(
        functools.partial(_peer_kernel, tb=tb, eb=eb),
        grid=(t // tb, n_exp // eb),
        in_specs=[pl.BlockSpec((tb, d), lambda ti, ei: (ti, 0)),
                  pl.BlockSpec((eb, d), lambda ti, ei: (ei, 0)),
                  pl.BlockSpec((d, eb), lambda ti, ei: (0, ei)),
                  gspec, gspec, gspec, gspec,
                  pl.BlockSpec((tb, d), lambda ti, ei: (ti, 0))],
        out_specs=pl.BlockSpec((tb, d), lambda ti, ei: (ti, 0)),
        out_shape=jax.ShapeDtypeStruct((t, d), F32),
        scratch_shapes=[pltpu.VMEM((d, tb), F32), pltpu.VMEM((eb, tb), md)],
        compiler_params=_cparams("parallel", "arbitrary"),
        name="peer_experts",
    )(hn, peer_u.astype(md), peer_v.T.astype(md), r2, e2, nsel, csel, h_res)
    return out.reshape(b, seq, d)


def kernel(x, mem, attn_norm_gain, w_in, q_norm_gain, k_norm_gain, pool_w, pool_scale, mem_norm_gain,
           w_mem_kv, mq_norm_gain, mk_norm_gain, w_out, ffn_norm_gain, peer_w_q, peer_sub_keys_1,
           peer_sub_keys_2, peer_u, peer_v):
    for l in range(attn_norm_gain.shape[0]):
        x = _layer(x, mem, attn_norm_gain[l], w_in[l], q_norm_gain[l], k_norm_gain[l], pool_w[l],
                   pool_scale[l], mem_norm_gain[l], w_mem_kv[l], mq_norm_gain[l], mk_norm_gain[l],
                   w_out[l], ffn_norm_gain[l], peer_w_q[l], peer_sub_keys_1[l], peer_sub_keys_2[l],
                   peer_u[l], peer_v[l])
    return x
```

```python
import functools

import numpy as np
import jax
import jax.numpy as jnp
from jax import lax
from jax.experimental import pallas as pl
from jax.experimental.pallas import tpu as pltpu

F32 = jnp.float32
I32 = jnp.int32
MXU_DTYPE = jnp.bfloat16

LANES = 128
VMEM_LIMIT = 56 * 1024 * 1024

EPS = 1e-6
ROPE_THETA = 10000.0
CHUNK = 64
A_HEADS = 8
HEAD_DIM = 128
IDX_HEADS = 16
IDX_DIM = 64
TOP_K_MAX = 256
POOL_WINDOWS = (2, 4, 8, 16)
M_HEADS = 4
PEER_HEADS = 8
N_KEYS = 128
PEER_TOPK = 16
INT_MIN = -(2 ** 31)
NEG_BIG = -1e30


def _cparams(*sem):
    return pltpu.CompilerParams(dimension_semantics=sem, vmem_limit_bytes=VMEM_LIMIT)


def _rms(x, g):
    ms = jnp.mean(x * x, axis=-1, keepdims=True)
    return x * lax.rsqrt(ms + EPS) * g


def _dot(a, b):
    return jnp.dot(a, b, preferred_element_type=F32)


def _dot_nt(a, b):
    return lax.dot_general(a, b, (((1,), (1,)), ((), ())), preferred_element_type=F32)


def _lane_iota(shape):
    return lax.broadcasted_iota(I32, shape, len(shape) - 1)


def _qk_kernel(x_ref, g_ref, w_ref, hg_ref, cos_ref, sin_ref, o_ref, xn_ref):
    j = pl.program_id(1)

    @pl.when(j == 0)
    def _():
        xn_ref[...] = _rms(x_ref[...], g_ref[...]).astype(xn_ref.dtype)

    z = _dot(xn_ref[...], w_ref[...])

    cosf = cos_ref[...]
    sinf = sin_ref[...]
    hg = hg_ref[0]
    for h in range(A_HEADS):
        sl = slice(h * HEAD_DIM, (h + 1) * HEAD_DIM)
        n = _rms(z[:, sl], hg)
        o_ref[:, sl] = (n * cosf + pltpu.roll(n, HEAD_DIM // 2, 1) * sinf).astype(o_ref.dtype)


def _rope64(xb, cos4, sin4, first_half):
    rot = jnp.where(first_half, pltpu.roll(xb, LANES - IDX_DIM // 2, 1), pltpu.roll(xb, IDX_DIM // 2, 1))
    return xb * cos4 + rot * sin4


def _split3(v):
    hi = v.astype(MXU_DTYPE)
    r1 = v - hi.astype(F32)
    mid = r1.astype(MXU_DTYPE)
    lo = (r1 - mid.astype(F32)).astype(MXU_DTYPE)
    return hi, mid, lo


def _aux_proj_kernel(x_ref, g_ref, wvt_ref, wq_ref, ws_ref, e_ref, cos_ref, sin_ref, wpu_ref, wqm_ref, mqg_ref,
                     vt_ref, qi_ref, kd_ref, wt_ref, pu_ref, qm_ref):
    xn = _rms(x_ref[...], g_ref[...]).astype(MXU_DTYPE)
    vt_ref[...] = _dot_nt(wvt_ref[...], xn).astype(vt_ref.dtype)
    pu_ref[...] = _dot(xn, wpu_ref[...]).astype(pu_ref.dtype)
    zm = _dot(xn, wqm_ref[...])
    mqg = mqg_ref[...]
    for h in range(M_HEADS):
        sl = slice(h * HEAD_DIM, (h + 1) * HEAD_DIM)
        qm_ref[:, sl] = _rms(zm[:, sl], mqg).astype(qm_ref.dtype)

    zq = _dot(xn, wq_ref[...])
    zs = _dot(xn, ws_ref[...])
    wt_ref[...] = zs.T
    hi, mid, lo = _split3(zs)
    e = e_ref[...]
    wexp = _dot(hi, e) + _dot(mid, e) + _dot(lo, e)
    cos4 = cos_ref[...]
    sin4 = sin_ref[...]
    lane = _lane_iota(zs.shape)
    first_half = (lane & (IDX_DIM // 2)) == 0
    scale = (IDX_DIM ** -0.5) * (IDX_HEADS ** -0.5)
    for c in range(zq.shape[1] // LANES):
        sl = slice(c * LANES, (c + 1) * LANES)
        y = _rope64(zq[:, sl], cos4, sin4, first_half)
        qi_ref[:, sl] = (y * (wexp[:, sl] * scale)).astype(qi_ref.dtype)
    yk = _rope64(zs, cos4, sin4, first_half)
    kd_ref[...] = jnp.where(lane < IDX_DIM, yk, pltpu.roll(yk, IDX_DIM, 1)).astype(kd_ref.dtype)


def _memkv_kernel(x_ref, g_ref, w_ref, hg_ref, o_ref):
    j = pl.program_id(0)
    xn = _rms(x_ref[...], g_ref[...]).astype(MXU_DTYPE)
    z = _dot(xn, w_ref[...])

    @pl.when(j == 0)
    def _():
        hg = hg_ref[...]
        for h in range(M_HEADS):
            sl = slice(h * HEAD_DIM, (h + 1) * HEAD_DIM)
            o_ref[:, sl] = _rms(z[:, sl], hg).astype(o_ref.dtype)

    @pl.when(j == 1)
    def _():
        o_ref[...] = z.astype(o_ref.dtype)


def _select_kernel(qi_ref, wt_ref, kd_ref, mask_ref, lhs_ref, lo_ref, hi_ref, key_ref, *, tq, tk, seq, top_k):
    qt = pl.program_id(1)
    t0 = qt * tq
    n_kt = (t0 + tq + tk - 1) // tk
    hgrp = 4
    rb = 128
    n_rb = tk // rb
    cr = 32

    lane = _lane_iota((tq, LANES))
    for h in range(IDX_HEADS):
        pair = qi_ref[:, (h // 2) * LANES:(h // 2 + 1) * LANES]
        own = (lane < IDX_DIM) if h % 2 == 0 else (lane >= IDX_DIM)
        lhs_ref[h * tq:(h + 1) * tq, :] = jnp.where(own, pair, jnp.zeros_like(pair))
    w = wt_ref[IDX_DIM:IDX_DIM + IDX_HEADS, :]
    lo_ref[...] = jnp.where(w > 0, 0.0, -jnp.inf).astype(F32)
    hi_ref[...] = jnp.where(w > 0, jnp.inf, 0.0).astype(F32)

    q_lim = ((t0 + _lane_iota((1, tq))) // CHUNK + 1) * CHUNK

    def rows(kt, r):
        return pl.ds(pl.multiple_of(kt * tk + r * rb, rb), rb)

    def score_tile(kt, carry):
        k0 = pl.multiple_of(kt * tk, tk)
        kd = kd_ref[pl.ds(k0, tk), :]
        accs = [jnp.zeros((rb, tq), F32) for _ in range(n_rb)]
        for g in range(IDX_HEADS // hgrp):
            raw = _dot_nt(kd, lhs_ref[g * hgrp * tq:(g + 1) * hgrp * tq, :])
            for hh in range(hgrp):
                h = g * hgrp + hh
                lo = lo_ref[h:h + 1, :]
                hi = hi_ref[h:h + 1, :]
                for r in range(n_rb):
                    blk = raw[r * rb:(r + 1) * rb, hh * tq:(hh + 1) * tq]
                    accs[r] = accs[r] + jnp.minimum(jnp.maximum(blk, lo), hi)
        for r in range(n_rb):
            bits = pltpu.bitcast(accs[r], I32)
            key = bits ^ ((bits >> 31) & 0x7FFFFFFF)
            key_ref[rows(kt, r), :] = jnp.where(k0 + r * rb + lax.broadcasted_iota(I32, (rb, tq), 0) < q_lim, key, INT_MIN)
        return carry

    lax.fori_loop(0, n_kt, score_tile, 0)

    def srow(kt, r):
        return kt * tk + r * rb + lax.broadcasted_iota(I32, (rb, tq), 0)

    def count_keys(ind_fn):
        def tile(kt, cnt):
            for r in range(n_rb):
                ind = ind_fn(key_ref[rows(kt, r), :], kt, r)
                for c in range(rb // cr):
                    cnt = cnt + ind[c * cr:(c + 1) * cr]
            return cnt

        def pair(p, cnt):
            return tile(2 * p + 1, tile(2 * p, cnt))

        cnt = lax.fori_loop(0, n_kt // 2, pair, jnp.zeros((cr, tq), I32))
        cnt = lax.fori_loop(n_kt // 2 * 2, n_kt, tile, cnt)
        return jnp.sum(cnt, axis=0, keepdims=True)

    def bit_step(i, carry):
        prefix, n_ge = carry
        cand_u = prefix | lax.shift_left(jnp.int32(1), jnp.int32(31) - i)
        cand = cand_u ^ INT_MIN
        total = count_keys(lambda key, kt, r: jnp.where(key >= cand, 1, 0))
        keep = total >= top_k
        return jnp.where(keep, cand_u, prefix), jnp.where(keep, total, n_ge)

    group = 4
    few = q_lim <= top_k

    def bit_group(state):
        g, prefix, n_ge, _ = state
        for j in range(group):
            prefix, n_ge = bit_step(g * group + j, (prefix, n_ge))
        open_ = jnp.logical_and(n_ge != top_k, jnp.logical_not(few))
        return g + 1, prefix, n_ge, jnp.max(open_.astype(I32))

    zero_row = jnp.zeros((1, tq), I32)
    state = (jnp.int32(0), zero_row, zero_row, jnp.int32(1))
    _, prefix, n_ge, _ = lax.while_loop(lambda st: jnp.logical_and(st[0] < 32 // group, st[3] > 0), bit_group, state)
    tau = prefix ^ INT_MIN

    def write_mask(sel_fn):
        def write_tile(kt, carry):
            for r in range(n_rb):
                key = key_ref[rows(kt, r), :]
                sel = jnp.logical_and(sel_fn(key, kt, r), key > INT_MIN)
                mask_ref[rows(kt, r), :] = sel.astype(mask_ref.dtype)
            return carry

        lax.fori_loop(0, n_kt, write_tile, 0)

    tied = n_ge > top_k
    any_tied = jnp.max(tied.astype(I32))

    @pl.when(any_tied == 0)
    def _():
        write_mask(lambda key, kt, r: key >= tau)

    @pl.when(any_tied > 0)
    def _():
        n_gt = count_keys(lambda key, kt, r: jnp.where(key > tau, 1, 0))
        need = jnp.where(tied, top_k - n_gt, 2 ** 31 - 1)
        idx_bits = max(1, (seq - 1).bit_length())

        def idx_step(i, p):
            cand = p | lax.shift_left(jnp.int32(1), jnp.int32(idx_bits - 1) - i)
            before = count_keys(lambda key, kt, r: jnp.where(jnp.logical_and(key == tau, srow(kt, r) < cand), 1, 0))
            return jnp.where(before < need, cand, p)

        cut = lax.fori_loop(0, idx_bits, idx_step, zero_row)
        write_mask(lambda key, kt, r: jnp.logical_or(key > tau, jnp.logical_and(key == tau, srow(kt, r) <= cut)))

    def zero_tile(kt, carry):
        mask_ref[pl.ds(pl.multiple_of(kt * tk, tk), tk), :] = jnp.zeros((tk, tq), mask_ref.dtype)
        return carry

    lax.fori_loop(n_kt, seq // tk, zero_tile, 0)


def _attn_kernel(qt_tab, kt_tab, q_ref, k_ref, vt_ref, mask_ref, o_ref, m_ref, l_ref, acc_ref, s_ref, *, tq, tk):
    step = pl.program_id(1)
    qt = qt_tab[step]
    kt = kt_tab[step]
    last_kt = ((qt + 1) * tq - 1) // tk

    @pl.when(kt == 0)
    def _():
        m_ref[...] = jnp.full(m_ref.shape, NEG_BIG, F32)
        l_ref[...] = jnp.zeros(l_ref.shape, F32)
        acc_ref[...] = jnp.zeros(acc_ref.shape, F32)

    bias = jnp.where(mask_ref[...].astype(I32) != 0, 0.0, -jnp.inf).astype(F32)
    hs = lambda h: slice(h * HEAD_DIM, (h + 1) * HEAD_DIM)

    def logits(h):
        s_ref[h % 2] = _dot_nt(k_ref[:, hs(h)], q_ref[:, hs(h)])

    logits(0)
    for h in range(A_HEADS):
        if h + 1 < A_HEADS:
            logits(h + 1)
        s = s_ref[h % 2] + bias
        m_prev = m_ref[h:h + 1, :]
        m_new = jnp.maximum(m_prev, jnp.max(s, axis=0, keepdims=True))
        alpha = jnp.exp2(m_prev - m_new)
        p = jnp.exp2(s - m_new)
        l_ref[h:h + 1, :] = alpha * l_ref[h:h + 1, :] + jnp.sum(p, axis=0, keepdims=True)
        acc_ref[hs(h), :] = alpha * acc_ref[hs(h), :] + _dot(vt_ref[hs(h), :], p.astype(MXU_DTYPE))
        m_ref[h:h + 1, :] = m_new

    @pl.when(kt == last_kt)
    def _():
        for h in range(A_HEADS):
            o_ref[:, hs(h)] = (acc_ref[hs(h), :] / l_ref[h:h + 1, :]).T.astype(o_ref.dtype)


def _mix_kernel(x_ref, a_ref, pu_ref, halo_ref, qm_ref, kvm_ref, pw_ref, ps_ref, wo_ref, fg_ref,
                h_ref, hn_ref, mix_ref, *, tm, seq):
    i = pl.program_id(0)
    tiles_per_seq = seq // tm
    pos0 = (i % tiles_per_seq) * tm
    halo_rows = halo_ref.shape[0]
    a_w = a_ref.shape[1]
    p_w = pu_ref.shape[1]
    grp = p_w // len(POOL_WINDOWS)

    halo = jnp.where(pos0 > 0, halo_ref[...], 0.0)
    ext = jnp.concatenate([halo, pu_ref[...]], axis=0)
    pos = pos0 + lax.broadcasted_iota(I32, (tm, grp), 0)
    run = ext
    width = 1
    for g, w in enumerate(POOL_WINDOWS):
        while width < w:
            run = run + pltpu.roll(run, width, 0)
            width *= 2
        sl = slice(g * grp, (g + 1) * grp)
        wsum = run[halo_rows:, sl]
        cnt = jnp.minimum(pos + 1, w).astype(F32)
        d = wsum / cnt - pu_ref[:, sl]
        pg = _dot(d.astype(MXU_DTYPE), pw_ref[g]) * ps_ref[:, sl]
        mix_ref[:, sl] = pg.astype(mix_ref.dtype)

    m_w = qm_ref.shape[1]
    for h in range(M_HEADS):
        sl = slice(h * HEAD_DIM, (h + 1) * HEAD_DIM)
        s = _dot_nt(qm_ref[:, sl], kvm_ref[:, sl])
        p = jnp.exp(s - jnp.max(s, axis=1, keepdims=True))
        o = _dot(p.astype(MXU_DTYPE), kvm_ref[:, m_w + h * HEAD_DIM:m_w + (h + 1) * HEAD_DIM])
        o = o / jnp.sum(p, axis=1, keepdims=True)
        mix_ref[:, p_w + h * HEAD_DIM:p_w + (h + 1) * HEAD_DIM] = o.astype(mix_ref.dtype)

    out = x_ref[...] + _dot(a_ref[...], wo_ref[0:a_w, :]) + _dot(mix_ref[...], wo_ref[a_w:, :])
    h_ref[...] = out
    hn_ref[...] = _rms(out, fg_ref[...]).astype(hn_ref.dtype)


def _top16_rows(s, row):
    rank = jnp.full(s.shape, float(PEER_TOPK), F32)
    big = float(s.shape[0])
    vals = []
    for k in range(PEER_TOPK):
        m = jnp.max(s, axis=0, keepdims=True)
        first = jnp.min(jnp.where(s == m, row, big), axis=0, keepdims=True)
        hit = row == first
        rank = jnp.where(hit, float(k), rank)
        s = jnp.where(hit, -jnp.inf, s)
        vals.append(m)
    return jnp.concatenate(vals, axis=0), rank


def _route_kernel(hn_ref, wq_ref, k1_ref, k2_ref, r2_ref, e2_ref, n_ref, c_ref, *, tm):
    q = _dot(hn_ref[...], wq_ref[...]).astype(MXU_DTYPE)
    row = lax.broadcasted_iota(I32, (N_KEYS, tm), 0).astype(F32)
    sub = lax.broadcasted_iota(I32, (8, tm), 0).astype(F32)
    half = N_KEYS
    for h in range(PEER_HEADS):
        s1 = _dot_nt(k1_ref[...], q[:, h * 2 * half:h * 2 * half + half])
        s2 = _dot_nt(k2_ref[...], q[:, h * 2 * half + half:(h + 1) * 2 * half])
        v1, rank1 = _top16_rows(s1, row)
        v2, rank2 = _top16_rows(s2, row)

        blocks, flats = [], []
        for half_a in range(2):
            blocks.append(v1[half_a * 8:(half_a + 1) * 8] + v2[0:1])
            flats.append((sub + half_a * 8) * PEER_TOPK)
        for b in range(1, 8):
            blk = v1[0:8] + v2[b:b + 1]
            blocks.append(jnp.where(sub < PEER_TOPK // (b + 1), blk, -jnp.inf))
            flats.append(sub * PEER_TOPK + b)
        blocks.append(v1[0:1] + v2[8:16])
        flats.append(sub + 8)
        cand = jnp.concatenate(blocks, axis=0)
        flat = jnp.concatenate(flats, axis=0)
        cand0 = cand
        sel = jnp.zeros(cand.shape, jnp.bool_)
        for _ in range(PEER_TOPK):
            m = jnp.max(cand, axis=0, keepdims=True)
            first = jnp.min(jnp.where(cand == m, flat, float(PEER_TOPK * PEER_TOPK)), axis=0, keepdims=True)
            hit = flat == first
            sel = jnp.logical_or(sel, hit)
            cand = jnp.where(hit, -jnp.inf, cand)
        top = v1[0:1] + v2[0:1]
        z = jnp.sum(jnp.where(sel, jnp.exp(cand0 - top), 0.0), axis=0, keepdims=True)
        self32 = sel.astype(F32)
        nb_lo = self32[0:8]
        for blk in range(2, 9):
            nb_lo = nb_lo + self32[blk * 8:(blk + 1) * 8]
        tail = jnp.sum(self32[72:80], axis=0, keepdims=True)
        nb_lo = nb_lo + jnp.where(sub == 0, tail, 0.0)
        nb = jnp.concatenate([nb_lo, self32[8:16]], axis=0)

        n_dense = jnp.zeros((N_KEYS, tm), F32)
        for a in range(PEER_TOPK):
            n_dense = jnp.where(rank1 == float(a), nb[a:a + 1], n_dense)
        r2_ref[h] = rank2
        e2_ref[h] = jnp.exp(s2 - v2[0:1])
        n_ref[h] = n_dense
        c_ref[h] = jnp.exp(s1 - v1[0:1]) / z


def _gelu(x):
    return 0.5 * x * (1.0 + lax.erf(x * (2.0 ** -0.5)))


def _peer_kernel(hn_ref, u_ref, vt_ref, r2_ref, e2_ref, n_ref, c_ref, h_ref, o_ref, acc_ref, at_ref, *, tb, eb):
    ei = pl.program_id(1)

    @pl.when(ei == 0)
    def _():
        acc_ref[...] = jnp.zeros(acc_ref.shape, F32)

    ht = _dot_nt(u_ref[...], hn_ref[...])
    ipb = eb // N_KEYS
    sub = 8
    base = pl.multiple_of((ei * ipb) // sub * sub, sub)
    off = (ei * ipb) % sub
    for il in range(ipb):
        for tcol in range(tb // LANES):
            cs = slice(tcol * LANES, (tcol + 1) * LANES)
            gate = jnp.zeros((N_KEYS, LANES), F32)
            for h in range(PEER_HEADS):
                n8 = n_ref[h, pl.ds(base, sub), cs]
                c8 = c_ref[h, pl.ds(base, sub), cs]
                nrow = n8[il:il + 1]
                crow = c8[il:il + 1]
                for o in range(ipb, sub, ipb):
                    nrow = jnp.where(off == o, n8[o + il:o + il + 1], nrow)
                    crow = jnp.where(off == o, c8[o + il:o + il + 1], crow)
                gate = gate + jnp.where(r2_ref[h, :, cs] < nrow, e2_ref[h, :, cs], 0.0) * crow
            a = gate * _gelu(ht[il * N_KEYS:(il + 1) * N_KEYS, cs])
            at_ref[il * N_KEYS:(il + 1) * N_KEYS, cs] = a.astype(at_ref.dtype)
    acc_ref[...] += _dot(vt_ref[...], at_ref[...])

    @pl.when(ei == pl.num_programs(1) - 1)
    def _():
        o_ref[...] = h_ref[...] + acc_ref[...].T


def _rope_tables(seq):
    pos = jnp.arange(seq, dtype=F32)[:, None]

    def table(half):
        inv = ROPE_THETA ** (-jnp.arange(half, dtype=F32) / half)
        ang = pos * inv[None, :]
        return jnp.cos(ang), jnp.sin(ang)

    c64, s64 = table(HEAD_DIM // 2)
    c32, s32 = table(IDX_DIM // 2)
    cos_a = jnp.concatenate([c64, c64], axis=1)
    sin_a = jnp.concatenate([-s64, s64], axis=1)
    cos_i = jnp.concatenate([c32, c32, c32, c32], axis=1)
    sin_i = jnp.concatenate([-s32, s32, -s32, s32], axis=1)
    return cos_a, sin_a, cos_i, sin_i


def _attn_steps(seq, tq, tk):
    qts, kts = [], []
    for qt in range(seq // tq):
        for kt in range(((qt + 1) * tq - 1) // tk + 1):
            qts.append(qt)
            kts.append(kt)
    return np.asarray(qts, np.int32), np.asarray(kts, np.int32)


def _full(shape):
    return pl.BlockSpec(shape, lambda *_: (0,) * len(shape))


def _layer(x, mem, attn_g, w_in, q_g, k_g, pool_w, pool_s, mem_g, w_mem_kv, mq_g, mk_g, w_out,
           ffn_g, peer_wq, sk1, sk2, peer_u, peer_v):
    b, seq, d = x.shape
    t = b * seq
    mem_len = mem.shape[1]
    a_w = A_HEADS * HEAD_DIM
    i_w = IDX_HEADS * IDX_DIM
    p_w = pool_w.shape[0] * pool_w.shape[1]
    m_w = M_HEADS * HEAD_DIM
    top_k = min(TOP_K_MAX, seq // 4)
    md = MXU_DTYPE

    x2 = x.reshape(t, d)
    row = lambda v: v.reshape(1, -1).astype(F32)
    cos_a, sin_a, cos_i, sin_i = _rope_tables(seq)

    o = 0
    w_qk = w_in[:, o:o + 2 * a_w].astype(md); o += 2 * a_w
    w_vt = w_in[:, o:o + a_w].T.astype(md); o += a_w
    w_qi = w_in[:, o:o + i_w].astype(md); o += i_w
    w_small = jnp.pad(w_in[:, o:o + IDX_DIM + IDX_HEADS], ((0, 0), (0, LANES - IDX_DIM - IDX_HEADS))).astype(md)
    o += IDX_DIM + IDX_HEADS
    w_pu = w_in[:, o:o + p_w].astype(md); o += p_w
    w_qm = w_in[:, o:o + m_w].astype(md)

    tm = min(1024, seq)
    n_rt = t // tm
    pos_blk = lambda i: i % (seq // tm)
    hg = jnp.stack([row(q_g) * (HEAD_DIM ** -0.5 * np.log2(np.e)), row(k_g)])
    qk = pl.pallas_call(
        _qk_kernel,
        grid=(n_rt, 2),
        in_specs=[pl.BlockSpec((tm, d), lambda i, j: (i, 0)),
                  pl.BlockSpec((1, d), lambda i, j: (0, 0)),
                  pl.BlockSpec((d, a_w), lambda i, j: (0, j)),
                  pl.BlockSpec((1, 1, HEAD_DIM), lambda i, j: (j, 0, 0)),
                  pl.BlockSpec((tm, HEAD_DIM), lambda i, j: (pos_blk(i), 0)),
                  pl.BlockSpec((tm, HEAD_DIM), lambda i, j: (pos_blk(i), 0))],
        out_specs=pl.BlockSpec((tm, a_w), lambda i, j: (i, j)),
        out_shape=jax.ShapeDtypeStruct((t, 2 * a_w), md),
        scratch_shapes=[pltpu.VMEM((tm, d), md)],
        compiler_params=_cparams("parallel", "arbitrary"),
        name="qk_proj",
    )(x2, row(attn_g), w_qk, hg, cos_a, sin_a)

    tmi = min(512, seq)
    expand = np.zeros((LANES, i_w), np.float32)
    for h in range(IDX_HEADS):
        expand[IDX_DIM + h, h * IDX_DIM:(h + 1) * IDX_DIM] = 1.0
    tok = lambda w: pl.BlockSpec((tmi, w), lambda i: (i, 0))
    tok_t = lambda w: pl.BlockSpec((w, tmi), lambda i: (0, i))
    rope_spec = pl.BlockSpec((tmi, LANES), lambda i: (i % (seq // tmi), 0))
    vt, qi, kd, wt, pu, qm = pl.pallas_call(
        _aux_proj_kernel,
        grid=(t // tmi,),
        in_specs=[tok(d), _full((1, d)), _full((a_w, d)), _full((d, i_w)), _full((d, LANES)), _full((LANES, i_w)),
                  rope_spec, rope_spec, _full((d, p_w)), _full((d, m_w)), _full((1, HEAD_DIM))],
        out_specs=[tok_t(a_w), tok(i_w), tok(LANES), tok_t(LANES), tok(p_w), tok(m_w)],
        out_shape=[jax.ShapeDtypeStruct((a_w, t), md),
                   jax.ShapeDtypeStruct((t, i_w), md),
                   jax.ShapeDtypeStruct((t, LANES), md),
                   jax.ShapeDtypeStruct((LANES, t), F32),
                   jax.ShapeDtypeStruct((t, p_w), F32),
                   jax.ShapeDtypeStruct((t, m_w), md)],
        compiler_params=_cparams("parallel"),
        name="aux_proj",
    )(x2, row(attn_g), w_vt, w_qi, w_small, jnp.asarray(expand, md), cos_i, sin_i, w_pu, w_qm,
      row(mq_g) * (HEAD_DIM ** -0.5))

    tmem = b * mem_len
    kvm = pl.pallas_call(
        _memkv_kernel,
        grid=(2,),
        in_specs=[_full((tmem, d)), _full((1, d)),
                  pl.BlockSpec((d, m_w), lambda j: (0, j)), _full((1, HEAD_DIM))],
        out_specs=pl.BlockSpec((tmem, m_w), lambda j: (0, j)),
        out_shape=jax.ShapeDtypeStruct((tmem, 2 * m_w), md),
        compiler_params=_cparams("arbitrary"),
        name="memkv_proj",
    )(mem.reshape(tmem, d), row(mem_g), w_mem_kv.astype(md), row(mk_g))

    tq, tks = 256, 512
    n_qt = seq // tq
    mask = pl.pallas_call(
        functools.partial(_select_kernel, tq=tq, tk=tks, seq=seq, top_k=top_k),
        grid=(b, n_qt),
        in_specs=[pl.BlockSpec((None, tq, i_w), lambda bi, qt: (bi, qt, 0)),
                  pl.BlockSpec((LANES, tq), lambda bi, qt: (0, bi * n_qt + qt)),
                  pl.BlockSpec((None, seq, LANES), lambda bi, qt: (bi, 0, 0))],
        out_specs=pl.BlockSpec((None, seq, tq), lambda bi, qt: (bi, 0, qt)),
        out_shape=jax.ShapeDtypeStruct((b, seq, seq), jnp.int8),
        scratch_shapes=[pltpu.VMEM((IDX_HEADS * tq, LANES), md),
                        pltpu.VMEM((IDX_HEADS, tq), F32),
                        pltpu.VMEM((IDX_HEADS, tq), F32),
                        pltpu.VMEM((seq, tq), I32)],
        compiler_params=_cparams("parallel", "arbitrary"),
        name="dsa_select",
    )(qi.reshape(b, seq, i_w), wt, kd.reshape(b, seq, LANES))

    tqa, tka = min(512, seq), min(1024, seq)
    n_kta = seq // tka
    qts, kts = _attn_steps(seq, tqa, tka)
    qk3 = qk.reshape(b, seq, 2 * a_w)
    a_out = pl.pallas_call(
        functools.partial(_attn_kernel, tq=tqa, tk=tka),
        grid_spec=pltpu.PrefetchScalarGridSpec(
            num_scalar_prefetch=2,
            grid=(b, len(qts)),
            in_specs=[pl.BlockSpec((None, tqa, a_w), lambda bi, s, qt, kt: (bi, qt[s], 0)),
                      pl.BlockSpec((None, tka, a_w), lambda bi, s, qt, kt: (bi, kt[s], 1)),
                      pl.BlockSpec((a_w, tka), lambda bi, s, qt, kt: (0, bi * n_kta + kt[s])),
                      pl.BlockSpec((None, tka, tqa), lambda bi, s, qt, kt: (bi, kt[s], qt[s]))],
            out_specs=pl.BlockSpec((None, tqa, a_w), lambda bi, s, qt, kt: (bi, qt[s], 0)),
            scratch_shapes=[pltpu.VMEM((A_HEADS, tqa), F32),
                            pltpu.VMEM((A_HEADS, tqa), F32),
                            pltpu.VMEM((a_w, tqa), F32),
                            pltpu.VMEM((2, tka, tqa), F32)]),
        out_shape=jax.ShapeDtypeStruct((b, seq, a_w), md),
        compiler_params=_cparams("parallel", "arbitrary"),
        name="dsa_attn",
    )(jnp.asarray(qts), jnp.asarray(kts), qk3, qk3, vt, mask)

    tmx = min(512, seq)
    halo_rows = POOL_WINDOWS[-1]
    hsub = tmx // halo_rows
    h_res, hn = pl.pallas_call(
        functools.partial(_mix_kernel, tm=tmx, seq=seq),
        grid=(t // tmx,),
        in_specs=[pl.BlockSpec((tmx, d), lambda i: (i, 0)),
                  pl.BlockSpec((tmx, a_w), lambda i: (i, 0)),
                  pl.BlockSpec((tmx, p_w), lambda i: (i, 0)),
                  pl.BlockSpec((halo_rows, p_w), lambda i: (jnp.maximum(i * hsub - 1, 0), 0)),
                  pl.BlockSpec((tmx, m_w), lambda i: (i, 0)),
                  pl.BlockSpec((mem_len, 2 * m_w), lambda i: (i // (seq // tmx), 0)),
                  _full(pool_w.shape), _full((1, p_w)), _full((d, d)), _full((1, d))],
        out_specs=[pl.BlockSpec((tmx, d), lambda i: (i, 0)),
                   pl.BlockSpec((tmx, d), lambda i: (i, 0))],
        out_shape=[jax.ShapeDtypeStruct((t, d), F32), jax.ShapeDtypeStruct((t, d), md)],
        scratch_shapes=[pltpu.VMEM((tmx, p_w + m_w), md)],
        compiler_params=_cparams("parallel"),
        name="mix_out",
    )(x2, a_out.reshape(t, a_w), pu, pu, qm, kvm, pool_w.astype(md), row(pool_s), w_out.astype(md), row(ffn_g))

    tmr = 256
    rshape = jax.ShapeDtypeStruct((PEER_HEADS, N_KEYS, t), F32)
    rspec = pl.BlockSpec((PEER_HEADS, N_KEYS, tmr), lambda i: (0, 0, i))
    r2, e2, nsel, csel = pl.pallas_call(
        functools.partial(_route_kernel, tm=tmr),
        grid=(t // tmr,),
        in_specs=[pl.BlockSpec((tmr, d), lambda i: (i, 0)),
                  _full(peer_wq.shape), _full(sk1.shape), _full(sk2.shape)],
        out_specs=[rspec, rspec, rspec, rspec],
        out_shape=[rshape, rshape, rshape, rshape],
        compiler_params=_cparams("parallel"),
        name="peer_route",
    )(hn, peer_wq.astype(md), sk1.astype(md), sk2.astype(md))

    tb, eb = 512, 512
    n_exp = peer_u.shape[0]
    gspec = pl.BlockSpec((PEER_HEADS, N_KEYS, tb), lambda ti, ei: (0, 0, ti))
    out = pl.pallas_call(
        functools.partial(_peer_kernel, tb=tb, eb=eb),
        grid=(t // tb, n_exp // eb),
        in_specs=[pl.BlockSpec((tb, d), lambda ti, ei: (ti, 0)),
                  pl.BlockSpec((eb, d), lambda ti, ei: (ei, 0)),
                  pl.BlockSpec((d, eb), lambda ti, ei: (0, ei)),
                  gspec, gspec, gspec, gspec,
                  pl.BlockSpec((tb, d), lambda ti, ei: (ti, 0))],
        out_specs=pl.BlockSpec((tb, d), lambda ti, ei: (ti, 0)),
        out_shape=jax.ShapeDtypeStruct((t, d), F32),
        scratch_shapes=[pltpu.VMEM((d, tb), F32), pltpu.VMEM((eb, tb), md)],
        compiler_params=_cparams("parallel", "arbitrary"),
        name="peer_experts",
    )(hn, peer_u.astype(md), peer_v.T.astype(md), r2, e2, nsel, csel, h_res)
    return out.reshape(b, seq, d)


def kernel(x, mem, attn_norm_gain, w_in, q_norm_gain, k_norm_gain, pool_w, pool_scale, mem_norm_gain,
           w_mem_kv, mq_norm_gain, mk_norm_gain, w_out, ffn_norm_gain, peer_w_q, peer_sub_keys_1,
           peer_sub_keys_2, peer_u, peer_v):
    for l in range(attn_norm_gain.shape[0]):
        x = _layer(x, mem, attn_norm_gain[l], w_in[l], q_norm_gain[l], k_norm_gain[l], pool_w[l],
                   pool_scale[l], mem_norm_gain[l], w_mem_kv[l], mq_norm_gain[l], mk_norm_gain[l],
                   w_out[l], ffn_norm_gain[l], peer_w_q[l], peer_sub_keys_1[l], peer_sub_keys_2[l],
                   peer_u[l], peer_v[l])
    return x
```

```python
import functools

import numpy as np
import jax
import jax.numpy as jnp
from jax import lax
from jax.experimental import pallas as pl
from jax.experimental.pallas import tpu as pltpu

F32 = jnp.float32
I32 = jnp.int32
MXU_DTYPE = jnp.bfloat16

LANES = 128
VMEM_LIMIT = 56 * 1024 * 1024

EPS = 1e-6
ROPE_THETA = 10000.0
CHUNK = 64
A_HEADS = 8
HEAD_DIM = 128
IDX_HEADS = 16
IDX_DIM = 64
TOP_K_MAX = 256
POOL_WINDOWS = (2, 4, 8, 16)
M_HEADS = 4
PEER_HEADS = 8
N_KEYS = 128
PEER_TOPK = 16
INT_MIN = -(2 ** 31)
NEG_BIG = -1e30


def _cparams(*sem):
    return pltpu.CompilerParams(dimension_semantics=sem, vmem_limit_bytes=VMEM_LIMIT)


def _rms(x, g):
    ms = jnp.mean(x * x, axis=-1, keepdims=True)
    return x * lax.rsqrt(ms + EPS) * g


def _dot(a, b):
    return jnp.dot(a, b, preferred_element_type=F32)


def _dot_nt(a, b):
    return lax.dot_general(a, b, (((1,), (1,)), ((), ())), preferred_element_type=F32)


def _lane_iota(shape):
    return lax.broadcasted_iota(I32, shape, len(shape) - 1)


def _qk_kernel(x_ref, g_ref, w_ref, hg_ref, cos_ref, sin_ref, o_ref, xn_ref):
    j = pl.program_id(1)

    @pl.when(j == 0)
    def _():
        xn_ref[...] = _rms(x_ref[...], g_ref[...]).astype(xn_ref.dtype)

    z = _dot(xn_ref[...], w_ref[...])

    cosf = cos_ref[...]
    sinf = sin_ref[...]
    hg = hg_ref[0]
    for h in range(A_HEADS):
        sl = slice(h * HEAD_DIM, (h + 1) * HEAD_DIM)
        n = _rms(z[:, sl], hg)
        o_ref[:, sl] = (n * cosf + pltpu.roll(n, HEAD_DIM // 2, 1) * sinf).astype(o_ref.dtype)


def _rope64(xb, cos4, sin4, first_half):
    rot = jnp.where(first_half, pltpu.roll(xb, LANES - IDX_DIM // 2, 1), pltpu.roll(xb, IDX_DIM // 2, 1))
    return xb * cos4 + rot * sin4


def _split3(v):
    hi = v.astype(MXU_DTYPE)
    r1 = v - hi.astype(F32)
    mid = r1.astype(MXU_DTYPE)
    lo = (r1 - mid.astype(F32)).astype(MXU_DTYPE)
    return hi, mid, lo


def _aux_proj_kernel(x_ref, g_ref, wvt_ref, wq_ref, ws_ref, e_ref, cos_ref, sin_ref, wpu_ref, wqm_ref, mqg_ref,
                     vt_ref, qi_ref, kd_ref, wt_ref, pu_ref, qm_ref):
    xn = _rms(x_ref[...], g_ref[...]).astype(MXU_DTYPE)
    vt_ref[...] = _dot_nt(wvt_ref[...], xn).astype(vt_ref.dtype)
    pu_ref[...] = _dot(xn, wpu_ref[...]).astype(pu_ref.dtype)
    zm = _dot(xn, wqm_ref[...])
    mqg = mqg_ref[...]
    for h in range(M_HEADS):
        sl = slice(h * HEAD_DIM, (h + 1) * HEAD_DIM)
        qm_ref[:, sl] = _rms(zm[:, sl], mqg).astype(qm_ref.dtype)

    zq = _dot(xn, wq_ref[...])
    zs = _dot(xn, ws_ref[...])
    wt_ref[...] = zs.T
    hi, mid, lo = _split3(zs)
    e = e_ref[...]
    wexp = _dot(hi, e) + _dot(mid, e) + _dot(lo, e)
    cos4 = cos_ref[...]
    sin4 = sin_ref[...]
    lane = _lane_iota(zs.shape)
    first_half = (lane & (IDX_DIM // 2)) == 0
    scale = (IDX_DIM ** -0.5) * (IDX_HEADS ** -0.5)
    for c in range(zq.shape[1] // LANES):
        sl = slice(c * LANES, (c + 1) * LANES)
        y = _rope64(zq[:, sl], cos4, sin4, first_half)
        qi_ref[:, sl] = (y * (wexp[:, sl] * scale)).astype(qi_ref.dtype)
    yk = _rope64(zs, cos4, sin4, first_half)
    kd_ref[...] = jnp.where(lane < IDX_DIM, yk, pltpu.roll(yk, IDX_DIM, 1)).astype(kd_ref.dtype)


def _memkv_kernel(x_ref, g_ref, w_ref, hg_ref, o_ref):
    j = pl.program_id(0)
    xn = _rms(x_ref[...], g_ref[...]).astype(MXU_DTYPE)
    z = _dot(xn, w_ref[...])

    @pl.when(j == 0)
    def _():
        hg = hg_ref[...]
        for h in range(M_HEADS):
            sl = slice(h * HEAD_DIM, (h + 1) * HEAD_DIM)
            o_ref[:, sl] = _rms(z[:, sl], hg).astype(o_ref.dtype)

    @pl.when(j == 1)
    def _():
        o_ref[...] = z.astype(o_ref.dtype)


def _select_kernel(qi_ref, wt_ref, kd_ref, mask_ref, lhs_ref, lo_ref, hi_ref, key_ref, *, tq, tk, seq, top_k):
    qt = pl.program_id(1)
    t0 = qt * tq
    n_kt = (t0 + tq + tk - 1) // tk
    hgrp = 4
    rb = 128
    n_rb = tk // rb
    cr = 32

    lane = _lane_iota((tq, LANES))
    for h in range(IDX_HEADS):
        pair = qi_ref[:, (h // 2) * LANES:(h // 2 + 1) * LANES]
        own = (lane < IDX_DIM) if h % 2 == 0 else (lane >= IDX_DIM)
        lhs_ref[h * tq:(h + 1) * tq, :] = jnp.where(own, pair, jnp.zeros_like(pair))
    w = wt_ref[IDX_DIM:IDX_DIM + IDX_HEADS, :]
    lo_ref[...] = jnp.where(w > 0, 0.0, -jnp.inf).astype(F32)
    hi_ref[...] = jnp.where(w > 0, jnp.inf, 0.0).astype(F32)

    q_lim = ((t0 + _lane_iota((1, tq))) // CHUNK + 1) * CHUNK

    def rows(kt, r):
        return pl.ds(pl.multiple_of(kt * tk + r * rb, rb), rb)

    def score_tile(kt, carry):
        k0 = pl.multiple_of(kt * tk, tk)
        kd = kd_ref[pl.ds(k0, tk), :]
        accs = [jnp.zeros((rb, tq), F32) for _ in range(n_rb)]
        for g in range(IDX_HEADS // hgrp):
            raw = _dot_nt(kd, lhs_ref[g * hgrp * tq:(g + 1) * hgrp * tq, :])
            for hh in range(hgrp):
                h = g * hgrp + hh
                lo = lo_ref[h:h + 1, :]
                hi = hi_ref[h:h + 1, :]
                for r in range(n_rb):
                    blk = raw[r * rb:(r + 1) * rb, hh * tq:(hh + 1) * tq]
                    accs[r] = accs[r] + jnp.minimum(jnp.maximum(blk, lo), hi)
        for r in range(n_rb):
            bits = pltpu.bitcast(accs[r], I32)
            key = bits ^ ((bits >> 31) & 0x7FFFFFFF)
            key_ref[rows(kt, r), :] = jnp.where(k0 + r * rb + lax.broadcasted_iota(I32, (rb, tq), 0) < q_lim, key, INT_MIN)
        return carry

    lax.fori_loop(0, n_kt, score_tile, 0)

    def srow(kt, r):
        return kt * tk + r * rb + lax.broadcasted_iota(I32, (rb, tq), 0)

    def count_keys(ind_fn):
        def tile(kt, cnt):
            for r in range(n_rb):
                ind = ind_fn(key_ref[rows(kt, r), :], kt, r)
                for c in range(rb // cr):
                    cnt = cnt + ind[c * cr:(c + 1) * cr]
            return cnt

        def pair(p, cnt):
            return tile(2 * p + 1, tile(2 * p, cnt))

        cnt = lax.fori_loop(0, n_kt // 2, pair, jnp.zeros((cr, tq), I32))
        cnt = lax.fori_loop(n_kt // 2 * 2, n_kt, tile, cnt)
        return jnp.sum(cnt, axis=0, keepdims=True)

    def bit_step(i, carry):
        prefix, n_ge = carry
        cand_u = prefix | lax.shift_left(jnp.int32(1), jnp.int32(31) - i)
        cand = cand_u ^ INT_MIN
        total = count_keys(lambda key, kt, r: jnp.where(key >= cand, 1, 0))
        keep = total >= top_k
        return jnp.where(keep, cand_u, prefix), jnp.where(keep, total, n_ge)

    group = 4
    few = q_lim <= top_k

    def bit_group(state):
        g, prefix, n_ge, _ = state
        for j in range(group):
            prefix, n_ge = bit_step(g * group + j, (prefix, n_ge))
        open_ = jnp.logical_and(n_ge != top_k, jnp.logical_not(few))
        return g + 1, prefix, n_ge, jnp.max(open_.astype(I32))

    zero_row = jnp.zeros((1, tq), I32)
    state = (jnp.int32(0), zero_row, zero_row, jnp.int32(1))
    _, prefix, n_ge, _ = lax.while_loop(lambda st: jnp.logical_and(st[0] < 32 // group, st[3] > 0), bit_group, state)
    tau = prefix ^ INT_MIN

    def write_mask(sel_fn):
        def write_tile(kt, carry):
            for r in range(n_rb):
                key = key_ref[rows(kt, r), :]
                sel = jnp.logical_and(sel_fn(key, kt, r), key > INT_MIN)
                mask_ref[rows(kt, r), :] = sel.astype(mask_ref.dtype)
            return carry

        lax.fori_loop(0, n_kt, write_tile, 0)

    tied = n_ge > top_k
    any_tied = jnp.max(tied.astype(I32))

    @pl.when(any_tied == 0)
    def _():
        write_mask(lambda key, kt, r: key >= tau)

    @pl.when(any_tied > 0)
    def _():
        n_gt = count_keys(lambda key, kt, r: jnp.where(key > tau, 1, 0))
        need = jnp.where(tied, top_k - n_gt, 2 ** 31 - 1)
        idx_bits = max(1, (seq - 1).bit_length())

        def idx_step(i, p):
            cand = p | lax.shift_left(jnp.int32(1), jnp.int32(idx_bits - 1) - i)
            before = count_keys(lambda key, kt, r: jnp.where(jnp.logical_and(key == tau, srow(kt, r) < cand), 1, 0))
            return jnp.where(before < need, cand, p)

        cut = lax.fori_loop(0, idx_bits, idx_step, zero_row)
        write_mask(lambda key, kt, r: jnp.logical_or(key > tau, jnp.logical_and(key == tau, srow(kt, r) <= cut)))

    def zero_tile(kt, carry):
        mask_ref[pl.ds(pl.multiple_of(kt * tk, tk), tk), :] = jnp.zeros((tk, tq), mask_ref.dtype)
        return carry

    lax.fori_loop(n_kt, seq // tk, zero_tile, 0)


def _attn_kernel(qt_tab, kt_tab, q_ref, k_ref, vt_ref, mask_ref, o_ref, m_ref, l_ref, acc_ref, s_ref, *, tq, tk):
    step = pl.program_id(1)
    qt = qt_tab[step]
    kt = kt_tab[step]
    last_kt = ((qt + 1) * tq - 1) // tk

    @pl.when(kt == 0)
    def _():
        m_ref[...] = jnp.full(m_ref.shape, NEG_BIG, F32)
        l_ref[...] = jnp.zeros(l_ref.shape, F32)
        acc_ref[...] = jnp.zeros(acc_ref.shape, F32)

    bias = jnp.where(mask_ref[...].astype(I32) != 0, 0.0, -jnp.inf).astype(F32)
    hs = lambda h: slice(h * HEAD_DIM, (h + 1) * HEAD_DIM)

    def logits(h):
        s_ref[h % 2] = _dot_nt(k_ref[:, hs(h)], q_ref[:, hs(h)])

    logits(0)
    for h in range(A_HEADS):
        if h + 1 < A_HEADS:
            logits(h + 1)
        s = s_ref[h % 2] + bias
        m_prev = m_ref[h:h + 1, :]
        m_new = jnp.maximum(m_prev, jnp.max(s, axis=0, keepdims=True))
        alpha = jnp.exp2(m_prev - m_new)
        p = jnp.exp2(s - m_new)
        l_ref[h:h + 1, :] = alpha * l_ref[h:h + 1, :] + jnp.sum(p, axis=0, keepdims=True)
        acc_ref[hs(h), :] = alpha * acc_ref[hs(h), :] + _dot(vt_ref[hs(h), :], p.astype(MXU_DTYPE))
        m_ref[h:h + 1, :] = m_new

    @pl.when(kt == last_kt)
    def _():
        for h in range(A_HEADS):
            o_ref[:, hs(h)] = (acc_ref[hs(h), :] / l_ref[h:h + 1, :]).T.astype(o_ref.dtype)


def _mix_kernel(x_ref, a_ref, pu_ref, halo_ref, qm_ref, kvm_ref, pw_ref, ps_ref, wo_ref, fg_ref,
                h_ref, hn_ref, mix_ref, *, tm, seq):
    i = pl.program_id(0)
    tiles_per_seq = seq // tm
    pos0 = (i % tiles_per_seq) * tm
    halo_rows = halo_ref.shape[0]
    a_w = a_ref.shape[1]
    p_w = pu_ref.shape[1]
    grp = p_w // len(POOL_WINDOWS)

    halo = jnp.where(pos0 > 0, halo_ref[...], 0.0)
    ext = jnp.concatenate([halo, pu_ref[...]], axis=0)
    pos = pos0 + lax.broadcasted_iota(I32, (tm, grp), 0)
    run = ext
    width = 1
    for g, w in enumerate(POOL_WINDOWS):
        while width < w:
            run = run + pltpu.roll(run, width, 0)
            width *= 2
        sl = slice(g * grp, (g + 1) * grp)
        wsum = run[halo_rows:, sl]
        cnt = jnp.minimum(pos + 1, w).astype(F32)
        d = wsum / cnt - pu_ref[:, sl]
        pg = _dot(d.astype(MXU_DTYPE), pw_ref[g]) * ps_ref[:, sl]
        mix_ref[:, sl] = pg.astype(mix_ref.dtype)

    m_w = qm_ref.shape[1]
    for h in range(M_HEADS):
        sl = slice(h * HEAD_DIM, (h + 1) * HEAD_DIM)
        s = _dot_nt(qm_ref[:, sl], kvm_ref[:, sl])
        p = jnp.exp(s - jnp.max(s, axis=1, keepdims=True))
        o = _dot(p.astype(MXU_DTYPE), kvm_ref[:, m_w + h * HEAD_DIM:m_w + (h + 1) * HEAD_DIM])
        o = o / jnp.sum(p, axis=1, keepdims=True)
        mix_ref[:, p_w + h * HEAD_DIM:p_w + (h + 1) * HEAD_DIM] = o.astype(mix_ref.dtype)

    out = x_ref[...] + _dot(a_ref[...], wo_ref[0:a_w, :]) + _dot(mix_ref[...], wo_ref[a_w:, :])
    h_ref[...] = out
    hn_ref[...] = _rms(out, fg_ref[...]).astype(hn_ref.dtype)


def _top16_rows(s, row):
    rank = jnp.full(s.shape, float(PEER_TOPK), F32)
    big = float(s.shape[0])
    vals = []
    for k in range(PEER_TOPK):
        m = jnp.max(s, axis=0, keepdims=True)
        first = jnp.min(jnp.where(s == m, row, big), axis=0, keepdims=True)
        hit = row == first
        rank = jnp.where(hit, float(k), rank)
        s = jnp.where(hit, -jnp.inf, s)
        vals.append(m)
    return jnp.concatenate(vals, axis=0), rank


def _route_kernel(hn_ref, wq_ref, k1_ref, k2_ref, r2_ref, e2_ref, n_ref, c_ref, *, tm):
    q = _dot(hn_ref[...], wq_ref[...]).astype(MXU_DTYPE)
    row = lax.broadcasted_iota(I32, (N_KEYS, tm), 0).astype(F32)
    sub = lax.broadcasted_iota(I32, (8, tm), 0).astype(F32)
    half = N_KEYS
    for h in range(PEER_HEADS):
        s1 = _dot_nt(k1_ref[...], q[:, h * 2 * half:h * 2 * half + half])
        s2 = _dot_nt(k2_ref[...], q[:, h * 2 * half + half:(h + 1) * 2 * half])
        v1, rank1 = _top16_rows(s1, row)
        v2, rank2 = _top16_rows(s2, row)

        blocks, flats = [], []
        for half_a in range(2):
            blocks.append(v1[half_a * 8:(half_a + 1) * 8] + v2[0:1])
            flats.append((sub + half_a * 8) * PEER_TOPK)
        for b in range(1, 8):
            blk = v1[0:8] + v2[b:b + 1]
            blocks.append(jnp.where(sub < PEER_TOPK // (b + 1), blk, -jnp.inf))
            flats.append(sub * PEER_TOPK + b)
        blocks.append(v1[0:1] + v2[8:16])
        flats.append(sub + 8)
        cand = jnp.concatenate(blocks, axis=0)
        flat = jnp.concatenate(flats, axis=0)
        cand0 = cand
        sel = jnp.zeros(cand.shape, jnp.bool_)
        for _ in range(PEER_TOPK):
            m = jnp.max(cand, axis=0, keepdims=True)
            first = jnp.min(jnp.where(cand == m, flat, float(PEER_TOPK * PEER_TOPK)), axis=0, keepdims=True)
            hit = flat == first
            sel = jnp.logical_or(sel, hit)
            cand = jnp.where(hit, -jnp.inf, cand)
        top = v1[0:1] + v2[0:1]
        z = jnp.sum(jnp.where(sel, jnp.exp(cand0 - top), 0.0), axis=0, keepdims=True)
        self32 = sel.astype(F32)
        nb_lo = self32[0:8]
        for blk in range(2, 9):
            nb_lo = nb_lo + self32[blk * 8:(blk + 1) * 8]
        tail = jnp.sum(self32[72:80], axis=0, keepdims=True)
        nb_lo = nb_lo + jnp.where(sub == 0, tail, 0.0)
        nb = jnp.concatenate([nb_lo, self32[8:16]], axis=0)

        n_dense = jnp.zeros((N_KEYS, tm), F32)
        for a in range(PEER_TOPK):
            n_dense = jnp.where(rank1 == float(a), nb[a:a + 1], n_dense)
        r2_ref[h] = rank2
        e2_ref[h] = jnp.exp(s2 - v2[0:1])
        n_ref[h] = n_dense
        c_ref[h] = jnp.exp(s1 - v1[0:1]) / z


def _gelu(x):
    return 0.5 * x * (1.0 + lax.erf(x * (2.0 ** -0.5)))


def _peer_kernel(hn_ref, u_ref, vt_ref, r2_ref, e2_ref, n_ref, c_ref, h_ref, o_ref, acc_ref, at_ref, *, tb, eb):
    ei = pl.program_id(1)

    @pl.when(ei == 0)
    def _():
        acc_ref[...] = jnp.zeros(acc_ref.shape, F32)

    ht = _dot_nt(u_ref[...], hn_ref[...])
    ipb = eb // N_KEYS
    sub = 8
    base = pl.multiple_of((ei * ipb) // sub * sub, sub)
    off = (ei * ipb) % sub
    for il in range(ipb):
        for tcol in range(tb // LANES):
            cs = slice(tcol * LANES, (tcol + 1) * LANES)
            gate = jnp.zeros((N_KEYS, LANES), F32)
            for h in range(PEER_HEADS):
                n8 = n_ref[h, pl.ds(base, sub), cs]
                c8 = c_ref[h, pl.ds(base, sub), cs]
                nrow = n8[il:il + 1]
                crow = c8[il:il + 1]
                for o in range(ipb, sub, ipb):
                    nrow = jnp.where(off == o, n8[o + il:o + il + 1], nrow)
                    crow = jnp.where(off == o, c8[o + il:o + il + 1], crow)
                gate = gate + jnp.where(r2_ref[h, :, cs] < nrow, e2_ref[h, :, cs], 0.0) * crow
            a = gate * _gelu(ht[il * N_KEYS:(il + 1) * N_KEYS, cs])
            at_ref[il * N_KEYS:(il + 1) * N_KEYS, cs] = a.astype(at_ref.dtype)
    acc_ref[...] += _dot(vt_ref[...], at_ref[...])

    @pl.when(ei == pl.num_programs(1) - 1)
    def _():
        o_ref[...] = h_ref[...] + acc_ref[...].T


def _rope_tables(seq):
    pos = jnp.arange(seq, dtype=F32)[:, None]

    def table(half):
        inv = ROPE_THETA ** (-jnp.arange(half, dtype=F32) / half)
        ang = pos * inv[None, :]
        return jnp.cos(ang), jnp.sin(ang)

    c64, s64 = table(HEAD_DIM // 2)
    c32, s32 = table(IDX_DIM // 2)
    cos_a = jnp.concatenate([c64, c64], axis=1)
    sin_a = jnp.concatenate([-s64, s64], axis=1)
    cos_i = jnp.concatenate([c32, c32, c32, c32], axis=1)
    sin_i = jnp.concatenate([-s32, s32, -s32, s32], axis=1)
    return cos_a, sin_a, cos_i, sin_i


def _attn_steps(seq, tq, tk):
    qts, kts = [], []
    for qt in range(seq // tq):
        for kt in range(((qt + 1) * tq - 1) // tk + 1):
            qts.append(qt)
            kts.append(kt)
    return np.asarray(qts, np.int32), np.asarray(kts, np.int32)


def _full(shape):
    return pl.BlockSpec(shape, lambda *_: (0,) * len(shape))


def _layer(x, mem, attn_g, w_in, q_g, k_g, pool_w, pool_s, mem_g, w_mem_kv, mq_g, mk_g, w_out,
           ffn_g, peer_wq, sk1, sk2, peer_u, peer_v):
    b, seq, d = x.shape
    t = b * seq
    mem_len = mem.shape[1]
    a_w = A_HEADS * HEAD_DIM
    i_w = IDX_HEADS * IDX_DIM
    p_w = pool_w.shape[0] * pool_w.shape[1]
    m_w = M_HEADS * HEAD_DIM
    top_k = min(TOP_K_MAX, seq // 4)
    md = MXU_DTYPE

    x2 = x.reshape(t, d)
    row = lambda v: v.reshape(1, -1).astype(F32)
    cos_a, sin_a, cos_i, sin_i = _rope_tables(seq)

    o = 0
    w_qk = w_in[:, o:o + 2 * a_w].astype(md); o += 2 * a_w
    w_vt = w_in[:, o:o + a_w].T.astype(md); o += a_w
    w_qi = w_in[:, o:o + i_w].astype(md); o += i_w
    w_small = jnp.pad(w_in[:, o:o + IDX_DIM + IDX_HEADS], ((0, 0), (0, LANES - IDX_DIM - IDX_HEADS))).astype(md)
    o += IDX_DIM + IDX_HEADS
    w_pu = w_in[:, o:o + p_w].astype(md); o += p_w
    w_qm = w_in[:, o:o + m_w].astype(md)

    tm = min(512, seq)
    n_rt = t // tm
    pos_blk = lambda i: i % (seq // tm)
    hg = jnp.stack([row(q_g) * (HEAD_DIM ** -0.5 * np.log2(np.e)), row(k_g)])
    qk = pl.pallas_call(
        _qk_kernel,
        grid=(n_rt, 2),
        in_specs=[pl.BlockSpec((tm, d), lambda i, j: (i, 0)),
                  pl.BlockSpec((1, d), lambda i, j: (0, 0)),
                  pl.BlockSpec((d, a_w), lambda i, j: (0, j)),
                  pl.BlockSpec((1, 1, HEAD_DIM), lambda i, j: (j, 0, 0)),
                  pl.BlockSpec((tm, HEAD_DIM), lambda i, j: (pos_blk(i), 0)),
                  pl.BlockSpec((tm, HEAD_DIM), lambda i, j: (pos_blk(i), 0))],
        out_specs=pl.BlockSpec((tm, a_w), lambda i, j: (i, j)),
        out_shape=jax.ShapeDtypeStruct((t, 2 * a_w), md),
        scratch_shapes=[pltpu.VMEM((tm, d), md)],
        compiler_params=_cparams("parallel", "arbitrary"),
        name="qk_proj",
    )(x2, row(attn_g), w_qk, hg, cos_a, sin_a)

    tmi = min(256, seq)
    expand = np.zeros((LANES, i_w), np.float32)
    for h in range(IDX_HEADS):
        expand[IDX_DIM + h, h * IDX_DIM:(h + 1) * IDX_DIM] = 1.0
    tok = lambda w: pl.BlockSpec((tmi, w), lambda i: (i, 0))
    tok_t = lambda w: pl.BlockSpec((w, tmi), lambda i: (0, i))
    rope_spec = pl.BlockSpec((tmi, LANES), lambda i: (i % (seq // tmi), 0))
    vt, qi, kd, wt, pu, qm = pl.pallas_call(
        _aux_proj_kernel,
        grid=(t // tmi,),
        in_specs=[tok(d), _full((1, d)), _full((a_w, d)), _full((d, i_w)), _full((d, LANES)), _full((LANES, i_w)),
                  rope_spec, rope_spec, _full((d, p_w)), _full((d, m_w)), _full((1, HEAD_DIM))],
        out_specs=[tok_t(a_w), tok(i_w), tok(LANES), tok_t(LANES), tok(p_w), tok(m_w)],
        out_shape=[jax.ShapeDtypeStruct((a_w, t), md),
                   jax.ShapeDtypeStruct((t, i_w), md),
                   jax.ShapeDtypeStruct((t, LANES), md),
                   jax.ShapeDtypeStruct((LANES, t), F32),
                   jax.ShapeDtypeStruct((t, p_w), F32),
                   jax.ShapeDtypeStruct((t, m_w), md)],
        compiler_params=_cparams("parallel"),
        name="aux_proj",
    )(x2, row(attn_g), w_vt, w_qi, w_small, jnp.asarray(expand, md), cos_i, sin_i, w_pu, w_qm,
      row(mq_g) * (HEAD_DIM ** -0.5))

    tmem = b * mem_len
    kvm = pl.pallas_call(
        _memkv_kernel,
        grid=(2,),
        in_specs=[_full((tmem, d)), _full((1, d)),
                  pl.BlockSpec((d, m_w), lambda j: (0, j)), _full((1, HEAD_DIM))],
        out_specs=pl.BlockSpec((tmem, m_w), lambda j: (0, j)),
        out_shape=jax.ShapeDtypeStruct((tmem, 2 * m_w), md),
        compiler_params=_cparams("arbitrary"),
        name="memkv_proj",
    )(mem.reshape(tmem, d), row(mem_g), w_mem_kv.astype(md), row(mk_g))

    tq, tks = 256, min(1024, seq)
    n_qt = seq // tq
    mask = pl.pallas_call(
        functools.partial(_select_kernel, tq=tq, tk=tks, seq=seq, top_k=top_k),
        grid=(b, n_qt),
        in_specs=[pl.BlockSpec((None, tq, i_w), lambda bi, qt: (bi, qt, 0)),
                  pl.BlockSpec((LANES, tq), lambda bi, qt: (0, bi * n_qt + qt)),
                  pl.BlockSpec((None, seq, LANES), lambda bi, qt: (bi, 0, 0))],
        out_specs=pl.BlockSpec((None, seq, tq), lambda bi, qt: (bi, 0, qt)),
        out_shape=jax.ShapeDtypeStruct((b, seq, seq), jnp.int8),
        scratch_shapes=[pltpu.VMEM((IDX_HEADS * tq, LANES), md),
                        pltpu.VMEM((IDX_HEADS, tq), F32),
                        pltpu.VMEM((IDX_HEADS, tq), F32),
                        pltpu.VMEM((seq, tq), I32)],
        compiler_params=_cparams("parallel", "arbitrary"),
        name="dsa_select",
    )(qi.reshape(b, seq, i_w), wt, kd.reshape(b, seq, LANES))

    tqa, tka = min(512, seq), min(1024, seq)
    n_kta = seq // tka
    qts, kts = _attn_steps(seq, tqa, tka)
    qk3 = qk.reshape(b, seq, 2 * a_w)
    a_out = pl.pallas_call(
        functools.partial(_attn_kernel, tq=tqa, tk=tka),
        grid_spec=pltpu.PrefetchScalarGridSpec(
            num_scalar_prefetch=2,
            grid=(b, len(qts)),
            in_specs=[pl.BlockSpec((None, tqa, a_w), lambda bi, s, qt, kt: (bi, qt[s], 0)),
                      pl.BlockSpec((None, tka, a_w), lambda bi, s, qt, kt: (bi, kt[s], 1)),
                      pl.BlockSpec((a_w, tka), lambda bi, s, qt, kt: (0, bi * n_kta + kt[s])),
                      pl.BlockSpec((None, tka, tqa), lambda bi, s, qt, kt: (bi, kt[s], qt[s]))],
            out_specs=pl.BlockSpec((None, tqa, a_w), lambda bi, s, qt, kt: (bi, qt[s], 0)),
            scratch_shapes=[pltpu.VMEM((A_HEADS, tqa), F32),
                            pltpu.VMEM((A_HEADS, tqa), F32),
                            pltpu.VMEM((a_w, tqa), F32),
                            pltpu.VMEM((2, tka, tqa), F32)]),
        out_shape=jax.ShapeDtypeStruct((b, seq, a_w), md),
        compiler_params=_cparams("parallel", "arbitrary"),
        name="dsa_attn",
    )(jnp.asarray(qts), jnp.asarray(kts), qk3, qk3, vt, mask)

    tmx = min(512, seq)
    halo_rows = POOL_WINDOWS[-1]
    hsub = tmx // halo_rows
    h_res, hn = pl.pallas_call(
        functools.partial(_mix_kernel, tm=tmx, seq=seq),
        grid=(t // tmx,),
        in_specs=[pl.BlockSpec((tmx, d), lambda i: (i, 0)),
                  pl.BlockSpec((tmx, a_w), lambda i: (i, 0)),
                  pl.BlockSpec((tmx, p_w), lambda i: (i, 0)),
                  pl.BlockSpec((halo_rows, p_w), lambda i: (jnp.maximum(i * hsub - 1, 0), 0)),
                  pl.BlockSpec((tmx, m_w), lambda i: (i, 0)),
                  pl.BlockSpec((mem_len, 2 * m_w), lambda i: (i // (seq // tmx), 0)),
                  _full(pool_w.shape), _full((1, p_w)), _full((d, d)), _full((1, d))],
        out_specs=[pl.BlockSpec((tmx, d), lambda i: (i, 0)),
                   pl.BlockSpec((tmx, d), lambda i: (i, 0))],
        out_shape=[jax.ShapeDtypeStruct((t, d), F32), jax.ShapeDtypeStruct((t, d), md)],
        scratch_shapes=[pltpu.VMEM((tmx, p_w + m_w), md)],
        compiler_params=_cparams("parallel"),
        name="mix_out",
    )(x2, a_out.reshape(t, a_w), pu, pu, qm, kvm, pool_w.astype(md), row(pool_s), w_out.astype(md), row(ffn_g))

    tmr = 256
    rshape = jax.ShapeDtypeStruct((PEER_HEADS, N_KEYS, t), F32)
    rspec = pl.BlockSpec((PEER_HEADS, N_KEYS, tmr), lambda i: (0, 0, i))
    r2, e2, nsel, csel = pl.pallas_call(
        functools.partial(_route_kernel, tm=tmr),
        grid=(t // tmr,),
        in_specs=[pl.BlockSpec((tmr, d), lambda i: (i, 0)),
                  _full(peer_wq.shape), _full(sk1.shape), _full(sk2.shape)],
        out_specs=[rspec, rspec, rspec, rspec],
        out_shape=[rshape, rshape, rshape, rshape],
        compiler_params=_cparams("parallel"),
        name="peer_route",
    )(hn, peer_wq.astype(md), sk1.astype(md), sk2.astype(md))

    tb, eb = 512, 512
    n_exp = peer_u.shape[0]
    gspec = pl.BlockSpec((PEER_HEADS, N_KEYS, tb), lambda ti, ei: (0, 0, ti))
    out = pl.pallas_call(
        functools.partial(_peer_kernel, tb=tb, eb=eb),
        grid=(t // tb, n_exp // eb),
        in_specs=[pl.BlockSpec((tb, d), lambda ti, ei: (ti, 0)),
                  pl.BlockSpec((eb, d), lambda ti, ei: (ei, 0)),
                  pl.BlockSpec((d, eb), lambda ti, ei: (0, ei)),
                  gspec, gspec, gspec, gspec,
                  pl.BlockSpec((tb, d), lambda ti, ei: (ti, 0))],
        out_specs=pl.BlockSpec((tb, d), lambda ti, ei: (ti, 0)),
        out_shape=jax.ShapeDtypeStruct((t, d), F32),
        scratch_shapes=[pltpu.VMEM((d, tb), F32), pltpu.VMEM((eb, tb), md)],
        compiler_params=_cparams("parallel", "arbitrary"),
        name="peer_experts",
    )(hn, peer_u.astype(md), peer_v.T.astype(md), r2, e2, nsel, csel, h_res)
    return out.reshape(b, seq, d)


def kernel(x, mem, attn_norm_gain, w_in, q_norm_gain, k_norm_gain, pool_w, pool_scale, mem_norm_gain,
           w_mem_kv, mq_norm_gain, mk_norm_gain, w_out, ffn_norm_gain, peer_w_q, peer_sub_keys_1,
           peer_sub_keys_2, peer_u, peer_v):
    for l in range(attn_norm_gain.shape[0]):
        x = _layer(x, mem, attn_norm_gain[l], w_in[l], q_norm_gain[l], k_norm_gain[l], pool_w[l],
                   pool_scale[l], mem_norm_gain[l], w_mem_kv[l], mq_norm_gain[l], mk_norm_gain[l],
                   w_out[l], ffn_norm_gain[l], peer_w_q[l], peer_sub_keys_1[l], peer_sub_keys_2[l],
                   peer_u[l], peer_v[l])
    return x
```

```python
import functools

import numpy as np
import jax
import jax.numpy as jnp
from jax import lax
from jax.experimental import pallas as pl
from jax.experimental.pallas import tpu as pltpu

F32 = jnp.float32
I32 = jnp.int32
MXU_DTYPE = jnp.bfloat16

LANES = 128
VMEM_LIMIT = 56 * 1024 * 1024

EPS = 1e-6
ROPE_THETA = 10000.0
CHUNK = 64
A_HEADS = 8
HEAD_DIM = 128
IDX_HEADS = 16
IDX_DIM = 64
TOP_K_MAX = 256
POOL_WINDOWS = (2, 4, 8, 16)
M_HEADS = 4
PEER_HEADS = 8
N_KEYS = 128
PEER_TOPK = 16
INT_MIN = -(2 ** 31)
NEG_BIG = -1e30


def _cparams(*sem):
    return pltpu.CompilerParams(dimension_semantics=sem, vmem_limit_bytes=VMEM_LIMIT)


def _rms(x, g):
    ms = jnp.mean(x * x, axis=-1, keepdims=True)
    return x * lax.rsqrt(ms + EPS) * g


def _dot(a, b):
    return jnp.dot(a, b, preferred_element_type=F32)


def _dot_nt(a, b):
    return lax.dot_general(a, b, (((1,), (1,)), ((), ())), preferred_element_type=F32)


def _lane_iota(shape):
    return lax.broadcasted_iota(I32, shape, len(shape) - 1)


def _qk_kernel(x_ref, g_ref, wqt_ref, wk_ref, gq_ref, gk_ref, cos_ref, sin_ref, cost_ref, sint_ref, qt_ref, k_ref):
    xn = _rms(x_ref[...], g_ref[...]).astype(MXU_DTYPE)
    zk = _dot(xn, wk_ref[...])
    cosf = cos_ref[...]
    sinf = sin_ref[...]
    gk = gk_ref[...]
    for h in range(A_HEADS):
        sl = slice(h * HEAD_DIM, (h + 1) * HEAD_DIM)
        n = _rms(zk[:, sl], gk)
        k_ref[:, sl] = (n * cosf + pltpu.roll(n, HEAD_DIM // 2, 1) * sinf).astype(k_ref.dtype)

    zq = _dot_nt(wqt_ref[...], xn)
    cost = cost_ref[...]
    sint = sint_ref[...]
    gq = gq_ref[...]
    half = HEAD_DIM // 2
    for h in range(A_HEADS):
        zh = zq[h * HEAD_DIM:(h + 1) * HEAD_DIM, :]
        ms = jnp.mean(zh * zh, axis=0, keepdims=True)
        n = zh * lax.rsqrt(ms + EPS) * gq
        rot = jnp.concatenate([n[half:], n[:half]], axis=0)
        qt_ref[h * HEAD_DIM:(h + 1) * HEAD_DIM, :] = (n * cost + rot * sint).astype(qt_ref.dtype)


def _rope64(xb, cos4, sin4, first_half):
    rot = jnp.where(first_half, pltpu.roll(xb, LANES - IDX_DIM // 2, 1), pltpu.roll(xb, IDX_DIM // 2, 1))
    return xb * cos4 + rot * sin4


def _split3(v):
    hi = v.astype(MXU_DTYPE)
    r1 = v - hi.astype(F32)
    mid = r1.astype(MXU_DTYPE)
    lo = (r1 - mid.astype(F32)).astype(MXU_DTYPE)
    return hi, mid, lo


def _aux_proj_kernel(x_ref, g_ref, wvt_ref, wq_ref, ws_ref, e_ref, cos_ref, sin_ref, wpu_ref, wqm_ref, mqg_ref,
                     vt_ref, qi_ref, kd_ref, wt_ref, pu_ref, qm_ref):
    xn = _rms(x_ref[...], g_ref[...]).astype(MXU_DTYPE)
    vt_ref[...] = _dot_nt(wvt_ref[...], xn).astype(vt_ref.dtype)
    pu_ref[...] = _dot(xn, wpu_ref[...]).astype(pu_ref.dtype)
    zm = _dot(xn, wqm_ref[...])
    mqg = mqg_ref[...]
    for h in range(M_HEADS):
        sl = slice(h * HEAD_DIM, (h + 1) * HEAD_DIM)
        qm_ref[:, sl] = _rms(zm[:, sl], mqg).astype(qm_ref.dtype)

    zq = _dot(xn, wq_ref[...])
    zs = _dot(xn, ws_ref[...])
    wt_ref[...] = zs.T
    hi, mid, lo = _split3(zs)
    e = e_ref[...]
    wexp = _dot(hi, e) + _dot(mid, e) + _dot(lo, e)
    cos4 = cos_ref[...]
    sin4 = sin_ref[...]
    lane = _lane_iota(zs.shape)
    first_half = (lane & (IDX_DIM // 2)) == 0
    scale = (IDX_DIM ** -0.5) * (IDX_HEADS ** -0.5)
    for c in range(zq.shape[1] // LANES):
        sl = slice(c * LANES, (c + 1) * LANES)
        y = _rope64(zq[:, sl], cos4, sin4, first_half)
        qi_ref[:, sl] = (y * (wexp[:, sl] * scale)).astype(qi_ref.dtype)
    yk = _rope64(zs, cos4, sin4, first_half)
    kd_ref[...] = jnp.where(lane < IDX_DIM, yk, pltpu.roll(yk, IDX_DIM, 1)).astype(kd_ref.dtype)


def _memkv_kernel(x_ref, g_ref, w_ref, hg_ref, o_ref):
    j = pl.program_id(0)
    xn = _rms(x_ref[...], g_ref[...]).astype(MXU_DTYPE)
    z = _dot(xn, w_ref[...])

    @pl.when(j == 0)
    def _():
        hg = hg_ref[...]
        for h in range(M_HEADS):
            sl = slice(h * HEAD_DIM, (h + 1) * HEAD_DIM)
            o_ref[:, sl] = _rms(z[:, sl], hg).astype(o_ref.dtype)

    @pl.when(j == 1)
    def _():
        o_ref[...] = z.astype(o_ref.dtype)


def _select_kernel(qi_ref, wt_ref, kd_ref, mask_ref, lhs_ref, lo_ref, hi_ref, key_ref, *, tq, tk, seq, top_k):
    qt = pl.program_id(1)
    t0 = qt * tq
    n_kt = (t0 + tq + tk - 1) // tk
    hgrp = 4
    rb = 128
    n_rb = tk // rb
    cr = 32

    lane = _lane_iota((tq, LANES))
    for h in range(IDX_HEADS):
        pair = qi_ref[:, (h // 2) * LANES:(h // 2 + 1) * LANES]
        own = (lane < IDX_DIM) if h % 2 == 0 else (lane >= IDX_DIM)
        lhs_ref[h * tq:(h + 1) * tq, :] = jnp.where(own, pair, jnp.zeros_like(pair))
    w = wt_ref[IDX_DIM:IDX_DIM + IDX_HEADS, :]
    lo_ref[...] = jnp.where(w > 0, 0.0, -jnp.inf).astype(F32)
    hi_ref[...] = jnp.where(w > 0, jnp.inf, 0.0).astype(F32)

    q_lim = ((t0 + _lane_iota((1, tq))) // CHUNK + 1) * CHUNK

    def rows(kt, r):
        return pl.ds(pl.multiple_of(kt * tk + r * rb, rb), rb)

    def score_tile(kt, carry):
        k0 = pl.multiple_of(kt * tk, tk)
        kd = kd_ref[pl.ds(k0, tk), :]
        accs = [jnp.zeros((rb, tq), F32) for _ in range(n_rb)]
        for g in range(IDX_HEADS // hgrp):
            raw = _dot_nt(kd, lhs_ref[g * hgrp * tq:(g + 1) * hgrp * tq, :])
            for hh in range(hgrp):
                h = g * hgrp + hh
                lo = lo_ref[h:h + 1, :]
                hi = hi_ref[h:h + 1, :]
                for r in range(n_rb):
                    blk = raw[r * rb:(r + 1) * rb, hh * tq:(hh + 1) * tq]
                    accs[r] = accs[r] + jnp.minimum(jnp.maximum(blk, lo), hi)
        for r in range(n_rb):
            bits = pltpu.bitcast(accs[r], I32)
            key = bits ^ ((bits >> 31) & 0x7FFFFFFF)
            key_ref[rows(kt, r), :] = jnp.where(k0 + r * rb + lax.broadcasted_iota(I32, (rb, tq), 0) < q_lim, key, INT_MIN)
        return carry

    lax.fori_loop(0, n_kt, score_tile, 0)

    def srow(kt, r):
        return kt * tk + r * rb + lax.broadcasted_iota(I32, (rb, tq), 0)

    def count_keys(ind_fn):
        def tile(kt, cnt):
            for r in range(n_rb):
                ind = ind_fn(key_ref[rows(kt, r), :], kt, r)
                for c in range(rb // cr):
                    cnt = cnt + ind[c * cr:(c + 1) * cr]
            return cnt

        def pair(p, cnt):
            return tile(2 * p + 1, tile(2 * p, cnt))

        cnt = lax.fori_loop(0, n_kt // 2, pair, jnp.zeros((cr, tq), I32))
        cnt = lax.fori_loop(n_kt // 2 * 2, n_kt, tile, cnt)
        return jnp.sum(cnt, axis=0, keepdims=True)

    def bit_step(i, carry):
        prefix, n_ge = carry
        cand_u = prefix | lax.shift_left(jnp.int32(1), jnp.int32(31) - i)
        cand = cand_u ^ INT_MIN
        total = count_keys(lambda key, kt, r: jnp.where(key >= cand, 1, 0))
        keep = total >= top_k
        return jnp.where(keep, cand_u, prefix), jnp.where(keep, total, n_ge)

    group = 4
    few = q_lim <= top_k

    def bit_group(state):
        g, prefix, n_ge, _ = state
        for j in range(group):
            prefix, n_ge = bit_step(g * group + j, (prefix, n_ge))
        open_ = jnp.logical_and(n_ge != top_k, jnp.logical_not(few))
        return g + 1, prefix, n_ge, jnp.max(open_.astype(I32))

    zero_row = jnp.zeros((1, tq), I32)
    state = (jnp.int32(0), zero_row, zero_row, jnp.int32(1))
    _, prefix, n_ge, _ = lax.while_loop(lambda st: jnp.logical_and(st[0] < 32 // group, st[3] > 0), bit_group, state)
    tau = prefix ^ INT_MIN

    def write_mask(sel_fn):
        def write_tile(kt, carry):
            for r in range(n_rb):
                key = key_ref[rows(kt, r), :]
                sel = jnp.logical_and(sel_fn(key, kt, r), key > INT_MIN)
                mask_ref[rows(kt, r), :] = sel.astype(mask_ref.dtype)
            return carry

        lax.fori_loop(0, n_kt, write_tile, 0)

    tied = n_ge > top_k
    any_tied = jnp.max(tied.astype(I32))

    @pl.when(any_tied == 0)
    def _():
        write_mask(lambda key, kt, r: key >= tau)

    @pl.when(any_tied > 0)
    def _():
        n_gt = count_keys(lambda key, kt, r: jnp.where(key > tau, 1, 0))
        need = jnp.where(tied, top_k - n_gt, 2 ** 31 - 1)
        idx_bits = max(1, (seq - 1).bit_length())

        def idx_step(i, p):
            cand = p | lax.shift_left(jnp.int32(1), jnp.int32(idx_bits - 1) - i)
            before = count_keys(lambda key, kt, r: jnp.where(jnp.logical_and(key == tau, srow(kt, r) < cand), 1, 0))
            return jnp.where(before < need, cand, p)

        cut = lax.fori_loop(0, idx_bits, idx_step, zero_row)
        write_mask(lambda key, kt, r: jnp.logical_or(key > tau, jnp.logical_and(key == tau, srow(kt, r) <= cut)))

    def zero_tile(kt, carry):
        mask_ref[pl.ds(pl.multiple_of(kt * tk, tk), tk), :] = jnp.zeros((tk, tq), mask_ref.dtype)
        return carry

    lax.fori_loop(n_kt, seq // tk, zero_tile, 0)


def _attn_kernel(qt_tab, kt_tab, q_ref, k_ref, vt_ref, mask_ref, o_ref, m_ref, l_ref, acc_ref, s_ref, *, tq, tk):
    step = pl.program_id(1)
    qt = qt_tab[step]
    kt = kt_tab[step]
    last_kt = ((qt + 1) * tq - 1) // tk

    @pl.when(kt == 0)
    def _():
        m_ref[...] = jnp.full(m_ref.shape, NEG_BIG, F32)
        l_ref[...] = jnp.zeros(l_ref.shape, F32)
        acc_ref[...] = jnp.zeros(acc_ref.shape, F32)

    bias = jnp.where(mask_ref[...].astype(I32) != 0, 0.0, -jnp.inf).astype(F32)
    hs = lambda h: slice(h * HEAD_DIM, (h + 1) * HEAD_DIM)

    def logits(h):
        s_ref[h % 2] = _dot(k_ref[:, hs(h)], q_ref[hs(h), :])

    logits(0)
    for h in range(A_HEADS):
        if h + 1 < A_HEADS:
            logits(h + 1)
        s = s_ref[h % 2] + bias
        m_prev = m_ref[h:h + 1, :]
        m_new = jnp.maximum(m_prev, jnp.max(s, axis=0, keepdims=True))
        alpha = jnp.exp2(m_prev - m_new)
        p = jnp.exp2(s - m_new)
        l_ref[h:h + 1, :] = alpha * l_ref[h:h + 1, :] + jnp.sum(p, axis=0, keepdims=True)
        acc_ref[hs(h), :] = alpha * acc_ref[hs(h), :] + _dot(vt_ref[hs(h), :], p.astype(MXU_DTYPE))
        m_ref[h:h + 1, :] = m_new

    @pl.when(kt == last_kt)
    def _():
        for h in range(A_HEADS):
            o_ref[:, hs(h)] = (acc_ref[hs(h), :] / l_ref[h:h + 1, :]).T.astype(o_ref.dtype)


def _mix_kernel(x_ref, a_ref, pu_ref, halo_ref, qm_ref, kvm_ref, pw_ref, ps_ref, wo_ref, fg_ref,
                h_ref, hn_ref, mix_ref, *, tm, seq):
    i = pl.program_id(0)
    tiles_per_seq = seq // tm
    pos0 = (i % tiles_per_seq) * tm
    halo_rows = halo_ref.shape[0]
    a_w = a_ref.shape[1]
    p_w = pu_ref.shape[1]
    grp = p_w // len(POOL_WINDOWS)

    halo = jnp.where(pos0 > 0, halo_ref[...], 0.0)
    ext = jnp.concatenate([halo, pu_ref[...]], axis=0)
    pos = pos0 + lax.broadcasted_iota(I32, (tm, grp), 0)
    run = ext
    width = 1
    for g, w in enumerate(POOL_WINDOWS):
        while width < w:
            run = run + pltpu.roll(run, width, 0)
            width *= 2
        sl = slice(g * grp, (g + 1) * grp)
        wsum = run[halo_rows:, sl]
        cnt = jnp.minimum(pos + 1, w).astype(F32)
        d = wsum / cnt - pu_ref[:, sl]
        pg = _dot(d.astype(MXU_DTYPE), pw_ref[g]) * ps_ref[:, sl]
        mix_ref[:, sl] = pg.astype(mix_ref.dtype)

    m_w = qm_ref.shape[1]
    for h in range(M_HEADS):
        sl = slice(h * HEAD_DIM, (h + 1) * HEAD_DIM)
        s = _dot_nt(qm_ref[:, sl], kvm_ref[:, sl])
        p = jnp.exp(s - jnp.max(s, axis=1, keepdims=True))
        o = _dot(p.astype(MXU_DTYPE), kvm_ref[:, m_w + h * HEAD_DIM:m_w + (h + 1) * HEAD_DIM])
        o = o / jnp.sum(p, axis=1, keepdims=True)
        mix_ref[:, p_w + h * HEAD_DIM:p_w + (h + 1) * HEAD_DIM] = o.astype(mix_ref.dtype)

    out = x_ref[...] + _dot(a_ref[...], wo_ref[0:a_w, :]) + _dot(mix_ref[...], wo_ref[a_w:, :])
    h_ref[...] = out
    hn_ref[...] = _rms(out, fg_ref[...]).astype(hn_ref.dtype)


def _top16_rows(s, row):
    rank = jnp.full(s.shape, float(PEER_TOPK), F32)
    big = float(s.shape[0])
    vals = []
    for k in range(PEER_TOPK):
        m = jnp.max(s, axis=0, keepdims=True)
        first = jnp.min(jnp.where(s == m, row, big), axis=0, keepdims=True)
        hit = row == first
        rank = jnp.where(hit, float(k), rank)
        s = jnp.where(hit, -jnp.inf, s)
        vals.append(m)
    return jnp.concatenate(vals, axis=0), rank


def _route_kernel(hn_ref, wq_ref, k1_ref, k2_ref, r2_ref, e2_ref, n_ref, c_ref, *, tm):
    q = _dot(hn_ref[...], wq_ref[...]).astype(MXU_DTYPE)
    row = lax.broadcasted_iota(I32, (N_KEYS, tm), 0).astype(F32)
    sub = lax.broadcasted_iota(I32, (8, tm), 0).astype(F32)
    half = N_KEYS
    for h in range(PEER_HEADS):
        s1 = _dot_nt(k1_ref[...], q[:, h * 2 * half:h * 2 * half + half])
        s2 = _dot_nt(k2_ref[...], q[:, h * 2 * half + half:(h + 1) * 2 * half])
        v1, rank1 = _top16_rows(s1, row)
        v2, rank2 = _top16_rows(s2, row)

        blocks, flats = [], []
        for half_a in range(2):
            blocks.append(v1[half_a * 8:(half_a + 1) * 8] + v2[0:1])
            flats.append((sub + half_a * 8) * PEER_TOPK)
        for b in range(1, 8):
            blk = v1[0:8] + v2[b:b + 1]
            blocks.append(jnp.where(sub < PEER_TOPK // (b + 1), blk, -jnp.inf))
            flats.append(sub * PEER_TOPK + b)
        blocks.append(v1[0:1] + v2[8:16])
        flats.append(sub + 8)
        cand = jnp.concatenate(blocks, axis=0)
        flat = jnp.concatenate(flats, axis=0)
        cand0 = cand
        sel = jnp.zeros(cand.shape, jnp.bool_)
        for _ in range(PEER_TOPK):
            m = jnp.max(cand, axis=0, keepdims=True)
            first = jnp.min(jnp.where(cand == m, flat, float(PEER_TOPK * PEER_TOPK)), axis=0, keepdims=True)
            hit = flat == first
            sel = jnp.logical_or(sel, hit)
            cand = jnp.where(hit, -jnp.inf, cand)
        top = v1[0:1] + v2[0:1]
        z = jnp.sum(jnp.where(sel, jnp.exp(cand0 - top), 0.0), axis=0, keepdims=True)
        self32 = sel.astype(F32)
        nb_lo = self32[0:8]
        for blk in range(2, 9):
            nb_lo = nb_lo + self32[blk * 8:(blk + 1) * 8]
        tail = jnp.sum(self32[72:80], axis=0, keepdims=True)
        nb_lo = nb_lo + jnp.where(sub == 0, tail, 0.0)
        nb = jnp.concatenate([nb_lo, self32[8:16]], axis=0)

        n_dense = jnp.zeros((N_KEYS, tm), F32)
        for a in range(PEER_TOPK):
            n_dense = jnp.where(rank1 == float(a), nb[a:a + 1], n_dense)
        r2_ref[h] = rank2
        e2_ref[h] = jnp.exp(s2 - v2[0:1])
        n_ref[h] = n_dense
        c_ref[h] = jnp.exp(s1 - v1[0:1]) / z


def _gelu(x):
    return 0.5 * x * (1.0 + lax.erf(x * (2.0 ** -0.5)))


def _peer_kernel(hn_ref, u_ref, vt_ref, r2_ref, e2_ref, n_ref, c_ref, h_ref, o_ref, acc_ref, at_ref, *, tb, eb):
    ei = pl.program_id(1)

    @pl.when(ei == 0)
    def _():
        acc_ref[...] = jnp.zeros(acc_ref.shape, F32)

    ht = _dot_nt(u_ref[...], hn_ref[...])
    ipb = eb // N_KEYS
    sub = 8
    base = pl.multiple_of((ei * ipb) // sub * sub, sub)
    off = (ei * ipb) % sub
    for il in range(ipb):
        for tcol in range(tb // LANES):
            cs = slice(tcol * LANES, (tcol + 1) * LANES)
            gate = jnp.zeros((N_KEYS, LANES), F32)
            for h in range(PEER_HEADS):
                n8 = n_ref[h, pl.ds(base, sub), cs]
                c8 = c_ref[h, pl.ds(base, sub), cs]
                nrow = n8[il:il + 1]
                crow = c8[il:il + 1]
                for o in range(ipb, sub, ipb):
                    nrow = jnp.where(off == o, n8[o + il:o + il + 1], nrow)
                    crow = jnp.where(off == o, c8[o + il:o + il + 1], crow)
                gate = gate + jnp.where(r2_ref[h, :, cs] < nrow, e2_ref[h, :, cs], 0.0) * crow
            a = gate * _gelu(ht[il * N_KEYS:(il + 1) * N_KEYS, cs])
            at_ref[il * N_KEYS:(il + 1) * N_KEYS, cs] = a.astype(at_ref.dtype)
    acc_ref[...] += _dot(vt_ref[...], at_ref[...])

    @pl.when(ei == pl.num_programs(1) - 1)
    def _():
        o_ref[...] = h_ref[...] + acc_ref[...].T


def _rope_tables(seq):
    pos = jnp.arange(seq, dtype=F32)[:, None]

    def table(half):
        inv = ROPE_THETA ** (-jnp.arange(half, dtype=F32) / half)
        ang = pos * inv[None, :]
        return jnp.cos(ang), jnp.sin(ang)

    c64, s64 = table(HEAD_DIM // 2)
    c32, s32 = table(IDX_DIM // 2)
    cos_a = jnp.concatenate([c64, c64], axis=1)
    sin_a = jnp.concatenate([-s64, s64], axis=1)
    cos_i = jnp.concatenate([c32, c32, c32, c32], axis=1)
    sin_i = jnp.concatenate([-s32, s32, -s32, s32], axis=1)
    return cos_a, sin_a, cos_i, sin_i


def _attn_steps(seq, tq, tk):
    qts, kts = [], []
    for qt in range(seq // tq):
        for kt in range(((qt + 1) * tq - 1) // tk + 1):
            qts.append(qt)
            kts.append(kt)
    return np.asarray(qts, np.int32), np.asarray(kts, np.int32)


def _full(shape):
    return pl.BlockSpec(shape, lambda *_: (0,) * len(shape))


def _layer(x, mem, attn_g, w_in, q_g, k_g, pool_w, pool_s, mem_g, w_mem_kv, mq_g, mk_g, w_out,
           ffn_g, peer_wq, sk1, sk2, peer_u, peer_v):
    b, seq, d = x.shape
    t = b * seq
    mem_len = mem.shape[1]
    a_w = A_HEADS * HEAD_DIM
    i_w = IDX_HEADS * IDX_DIM
    p_w = pool_w.shape[0] * pool_w.shape[1]
    m_w = M_HEADS * HEAD_DIM
    top_k = min(TOP_K_MAX, seq // 4)
    md = MXU_DTYPE

    x2 = x.reshape(t, d)
    row = lambda v: v.reshape(1, -1).astype(F32)
    cos_a, sin_a, cos_i, sin_i = _rope_tables(seq)

    o = 0
    w_qt = w_in[:, o:o + a_w].T.astype(md); o += a_w
    w_k = w_in[:, o:o + a_w].astype(md); o += a_w
    w_vt = w_in[:, o:o + a_w].T.astype(md); o += a_w
    w_qi = w_in[:, o:o + i_w].astype(md); o += i_w
    w_small = jnp.pad(w_in[:, o:o + IDX_DIM + IDX_HEADS], ((0, 0), (0, LANES - IDX_DIM - IDX_HEADS))).astype(md)
    o += IDX_DIM + IDX_HEADS
    w_pu = w_in[:, o:o + p_w].astype(md); o += p_w
    w_qm = w_in[:, o:o + m_w].astype(md)

    tm = min(512, seq)
    n_rt = t // tm
    pos_blk = lambda i: i % (seq // tm)
    gq = jnp.broadcast_to((q_g.astype(F32) * (HEAD_DIM ** -0.5 * np.log2(np.e)))[:, None], (HEAD_DIM, tm))
    rope_rows = pl.BlockSpec((tm, HEAD_DIM), lambda i: (pos_blk(i), 0))
    rope_cols = pl.BlockSpec((HEAD_DIM, tm), lambda i: (0, pos_blk(i)))
    qt, kk = pl.pallas_call(
        _qk_kernel,
        grid=(n_rt,),
        in_specs=[pl.BlockSpec((tm, d), lambda i: (i, 0)), _full((1, d)), _full((a_w, d)), _full((d, a_w)),
                  _full((HEAD_DIM, tm)), _full((1, HEAD_DIM)), rope_rows, rope_rows, rope_cols, rope_cols],
        out_specs=[pl.BlockSpec((a_w, tm), lambda i: (0, i)), pl.BlockSpec((tm, a_w), lambda i: (i, 0))],
        out_shape=[jax.ShapeDtypeStruct((a_w, t), md), jax.ShapeDtypeStruct((t, a_w), md)],
        compiler_params=_cparams("parallel"),
        name="qk_proj",
    )(x2, row(attn_g), w_qt, w_k, gq, row(k_g), cos_a, sin_a, cos_a.T, sin_a.T)

    tmi = min(256, seq)
    expand = np.zeros((LANES, i_w), np.float32)
    for h in range(IDX_HEADS):
        expand[IDX_DIM + h, h * IDX_DIM:(h + 1) * IDX_DIM] = 1.0
    tok = lambda w: pl.BlockSpec((tmi, w), lambda i: (i, 0))
    tok_t = lambda w: pl.BlockSpec((w, tmi), lambda i: (0, i))
    rope_spec = pl.BlockSpec((tmi, LANES), lambda i: (i % (seq // tmi), 0))
    vt, qi, kd, wt, pu, qm = pl.pallas_call(
        _aux_proj_kernel,
        grid=(t // tmi,),
        in_specs=[tok(d), _full((1, d)), _full((a_w, d)), _full((d, i_w)), _full((d, LANES)), _full((LANES, i_w)),
                  rope_spec, rope_spec, _full((d, p_w)), _full((d, m_w)), _full((1, HEAD_DIM))],
        out_specs=[tok_t(a_w), tok(i_w), tok(LANES), tok_t(LANES), tok(p_w), tok(m_w)],
        out_shape=[jax.ShapeDtypeStruct((a_w, t), md),
                   jax.ShapeDtypeStruct((t, i_w), md),
                   jax.ShapeDtypeStruct((t, LANES), md),
                   jax.ShapeDtypeStruct((LANES, t), F32),
                   jax.ShapeDtypeStruct((t, p_w), F32),
                   jax.ShapeDtypeStruct((t, m_w), md)],
        compiler_params=_cparams("parallel"),
        name="aux_proj",
    )(x2, row(attn_g), w_vt, w_qi, w_small, jnp.asarray(expand, md), cos_i, sin_i, w_pu, w_qm,
      row(mq_g) * (HEAD_DIM ** -0.5))

    tmem = b * mem_len
    kvm = pl.pallas_call(
        _memkv_kernel,
        grid=(2,),
        in_specs=[_full((tmem, d)), _full((1, d)),
                  pl.BlockSpec((d, m_w), lambda j: (0, j)), _full((1, HEAD_DIM))],
        out_specs=pl.BlockSpec((tmem, m_w), lambda j: (0, j)),
        out_shape=jax.ShapeDtypeStruct((tmem, 2 * m_w), md),
        compiler_params=_cparams("arbitrary"),
        name="memkv_proj",
    )(mem.reshape(tmem, d), row(mem_g), w_mem_kv.astype(md), row(mk_g))

    tq, tks = 256, min(1024, seq)
    n_qt = seq // tq
    mask = pl.pallas_call(
        functools.partial(_select_kernel, tq=tq, tk=tks, seq=seq, top_k=top_k),
        grid=(b, n_qt),
        in_specs=[pl.BlockSpec((None, tq, i_w), lambda bi, qt: (bi, qt, 0)),
                  pl.BlockSpec((LANES, tq), lambda bi, qt: (0, bi * n_qt + qt)),
                  pl.BlockSpec((None, seq, LANES), lambda bi, qt: (bi, 0, 0))],
        out_specs=pl.BlockSpec((None, seq, tq), lambda bi, qt: (bi, 0, qt)),
        out_shape=jax.ShapeDtypeStruct((b, seq, seq), jnp.int8),
        scratch_shapes=[pltpu.VMEM((IDX_HEADS * tq, LANES), md),
                        pltpu.VMEM((IDX_HEADS, tq), F32),
                        pltpu.VMEM((IDX_HEADS, tq), F32),
                        pltpu.VMEM((seq, tq), I32)],
        compiler_params=_cparams("parallel", "arbitrary"),
        name="dsa_select",
    )(qi.reshape(b, seq, i_w), wt, kd.reshape(b, seq, LANES))

    tqa, tka = min(512, seq), min(1024, seq)
    n_kta = seq // tka
    qts, kts = _attn_steps(seq, tqa, tka)
    n_qta = seq // tqa
    a_out = pl.pallas_call(
        functools.partial(_attn_kernel, tq=tqa, tk=tka),
        grid_spec=pltpu.PrefetchScalarGridSpec(
            num_scalar_prefetch=2,
            grid=(b, len(qts)),
            in_specs=[pl.BlockSpec((a_w, tqa), lambda bi, s, qt, kt: (0, bi * n_qta + qt[s])),
                      pl.BlockSpec((None, tka, a_w), lambda bi, s, qt, kt: (bi, kt[s], 0)),
                      pl.BlockSpec((a_w, tka), lambda bi, s, qt, kt: (0, bi * n_kta + kt[s])),
                      pl.BlockSpec((None, tka, tqa), lambda bi, s, qt, kt: (bi, kt[s], qt[s]))],
            out_specs=pl.BlockSpec((None, tqa, a_w), lambda bi, s, qt, kt: (bi, qt[s], 0)),
            scratch_shapes=[pltpu.VMEM((A_HEADS, tqa), F32),
                            pltpu.VMEM((A_HEADS, tqa), F32),
                            pltpu.VMEM((a_w, tqa), F32),
                            pltpu.VMEM((2, tka, tqa), F32)]),
        out_shape=jax.ShapeDtypeStruct((b, seq, a_w), md),
        compiler_params=_cparams("parallel", "arbitrary"),
        name="dsa_attn",
    )(jnp.asarray(qts), jnp.asarray(kts), qt, kk.reshape(b, seq, a_w), vt, mask)

    tmx = min(512, seq)
    halo_rows = POOL_WINDOWS[-1]
    hsub = tmx // halo_rows
    h_res, hn = pl.pallas_call(
        functools.partial(_mix_kernel, tm=tmx, seq=seq),
        grid=(t // tmx,),
        in_specs=[pl.BlockSpec((tmx, d), lambda i: (i, 0)),
                  pl.BlockSpec((tmx, a_w), lambda i: (i, 0)),
                  pl.BlockSpec((tmx, p_w), lambda i: (i, 0)),
                  pl.BlockSpec((halo_rows, p_w), lambda i: (jnp.maximum(i * hsub - 1, 0), 0)),
                  pl.BlockSpec((tmx, m_w), lambda i: (i, 0)),
                  pl.BlockSpec((mem_len, 2 * m_w), lambda i: (i // (seq // tmx), 0)),
                  _full(pool_w.shape), _full((1, p_w)), _full((d, d)), _full((1, d))],
        out_specs=[pl.BlockSpec((tmx, d), lambda i: (i, 0)),
                   pl.BlockSpec((tmx, d), lambda i: (i, 0))],
        out_shape=[jax.ShapeDtypeStruct((t, d), F32), jax.ShapeDtypeStruct((t, d), md)],
        scratch_shapes=[pltpu.VMEM((tmx, p_w + m_w), md)],
        compiler_params=_cparams("parallel"),
        name="mix_out",
    )(x2, a_out.reshape(t, a_w), pu, pu, qm, kvm, pool_w.astype(md), row(pool_s), w_out.astype(md), row(ffn_g))

    tmr = 256
    rshape = jax.ShapeDtypeStruct((PEER_HEADS, N_KEYS, t), F32)
    rspec = pl.BlockSpec((PEER_HEADS, N_KEYS, tmr), lambda i: (0, 0, i))
    r2, e2, nsel, csel = pl.pallas_call(
        functools.partial(_route_kernel, tm=tmr),
        grid=(t // tmr,),
        in_specs=[pl.BlockSpec((tmr, d), lambda i: (i, 0)),
                  _full(peer_wq.shape), _full(sk1.shape), _full(sk2.shape)],
        out_specs=[rspec, rspec, rspec, rspec],
        out_shape=[rshape, rshape, rshape, rshape],
        compiler_params=_cparams("parallel"),
        name="peer_route",
    )(hn, peer_wq.astype(md), sk1.astype(md), sk2.astype(md))

    tb, eb = 512, 512
    n_exp = peer_u.shape[0]
    gspec = pl.BlockSpec((PEER_HEADS, N_KEYS, tb), lambda ti, ei: (0, 0, ti))
    out = pl.pallas_call(
        functools.partial(_peer_kernel, tb=tb, eb=eb),
        grid=(t // tb, n_exp // eb),
        in_specs=[pl.BlockSpec((tb, d), lambda ti, ei: (ti, 0)),
                  pl.BlockSpec((eb, d), lambda ti, ei: (ei, 0)),
                  pl.BlockSpec((d, eb), lambda ti, ei: (0, ei)),
                  gspec, gspec, gspec, gspec,
                  pl.BlockSpec((tb, d), lambda ti, ei: (ti, 0))],
        out_specs=pl.BlockSpec((tb, d), lambda ti, ei: (ti, 0)),
        out_shape=jax.ShapeDtypeStruct((t, d), F32),
        scratch_shapes=[pltpu.VMEM((d, tb), F32), pltpu.VMEM((eb, tb), md)],
        compiler_params=_cparams("parallel", "arbitrary"),
        name="peer_experts",
    )(hn, peer_u.astype(md), peer_v.T.astype(md), r2, e2, nsel, csel, h_res)
    return out.reshape(b, seq, d)


def kernel(x, mem, attn_norm_gain, w_in, q_norm_gain, k_norm_gain, pool_w, pool_scale, mem_norm_gain,
           w_mem_kv, mq_norm_gain, mk_norm_gain, w_out, ffn_norm_gain, peer_w_q, peer_sub_keys_1,
           peer_sub_keys_2, peer_u, peer_v):
    for l in range(attn_norm_gain.shape[0]):
        x = _layer(x, mem, attn_norm_gain[l], w_in[l], q_norm_gain[l], k_norm_gain[l], pool_w[l],
                   pool_scale[l], mem_norm_gain[l], w_mem_kv[l], mq_norm_gain[l], mk_norm_gain[l],
                   w_out[l], ffn_norm_gain[l], peer_w_q[l], peer_sub_keys_1[l], peer_sub_keys_2[l],
                   peer_u[l], peer_v[l])
    return x
```

```python
import functools

import numpy as np
import jax
import jax.numpy as jnp
from jax import lax
from jax.experimental import pallas as pl
from jax.experimental.pallas import tpu as pltpu

F32 = jnp.float32
I32 = jnp.int32
MXU_DTYPE = jnp.bfloat16

LANES = 128
VMEM_LIMIT = 56 * 1024 * 1024

EPS = 1e-6
ROPE_THETA = 10000.0
CHUNK = 64
A_HEADS = 8
HEAD_DIM = 128
IDX_HEADS = 16
IDX_DIM = 64
TOP_K_MAX = 256
POOL_WINDOWS = (2, 4, 8, 16)
M_HEADS = 4
PEER_HEADS = 8
N_KEYS = 128
PEER_TOPK = 16
INT_MIN = -(2 ** 31)
NEG_BIG = -1e30


def _cparams(*sem):
    return pltpu.CompilerParams(dimension_semantics=sem, vmem_limit_bytes=VMEM_LIMIT)


def _rms(x, g):
    ms = jnp.mean(x * x, axis=-1, keepdims=True)
    return x * lax.rsqrt(ms + EPS) * g


def _dot(a, b):
    return jnp.dot(a, b, preferred_element_type=F32)


def _dot_nt(a, b):
    return lax.dot_general(a, b, (((1,), (1,)), ((), ())), preferred_element_type=F32)


def _lane_iota(shape):
    return lax.broadcasted_iota(I32, shape, len(shape) - 1)


def _qk_kernel(x_ref, g_ref, wqt_ref, wk_ref, gq_ref, gk_ref, cos_ref, sin_ref, cost_ref, sint_ref, qt_ref, k_ref):
    xn = _rms(x_ref[...], g_ref[...]).astype(MXU_DTYPE)
    zk = _dot(xn, wk_ref[...])
    cosf = cos_ref[...]
    sinf = sin_ref[...]
    gk = gk_ref[...]
    for h in range(A_HEADS):
        sl = slice(h * HEAD_DIM, (h + 1) * HEAD_DIM)
        n = _rms(zk[:, sl], gk)
        k_ref[:, sl] = (n * cosf + pltpu.roll(n, HEAD_DIM // 2, 1) * sinf).astype(k_ref.dtype)

    zq = _dot_nt(wqt_ref[...], xn)
    cost = cost_ref[...]
    sint = sint_ref[...]
    gq = gq_ref[...]
    half = HEAD_DIM // 2
    for h in range(A_HEADS):
        zh = zq[h * HEAD_DIM:(h + 1) * HEAD_DIM, :]
        ms = jnp.mean(zh * zh, axis=0, keepdims=True)
        n = zh * lax.rsqrt(ms + EPS) * gq
        rot = jnp.concatenate([n[half:], n[:half]], axis=0)
        qt_ref[h * HEAD_DIM:(h + 1) * HEAD_DIM, :] = (n * cost + rot * sint).astype(qt_ref.dtype)


def _rope64(xb, cos4, sin4, first_half):
    rot = jnp.where(first_half, pltpu.roll(xb, LANES - IDX_DIM // 2, 1), pltpu.roll(xb, IDX_DIM // 2, 1))
    return xb * cos4 + rot * sin4


def _aux_proj_kernel(x_ref, g_ref, wvt_ref, wq_ref, ws_ref, cos_ref, sin_ref, cost_ref, sint_ref, wpu_ref, wqm_ref, mqg_ref,
                     vt_ref, qi_ref, kd_ref, wt_ref, pu_ref, qm_ref):
    xn = _rms(x_ref[...], g_ref[...]).astype(MXU_DTYPE)
    vt_ref[...] = _dot_nt(wvt_ref[...], xn).astype(vt_ref.dtype)
    pu_ref[...] = _dot(xn, wpu_ref[...]).astype(pu_ref.dtype)
    zm = _dot(xn, wqm_ref[...])
    mqg = mqg_ref[...]
    for h in range(M_HEADS):
        sl = slice(h * HEAD_DIM, (h + 1) * HEAD_DIM)
        qm_ref[:, sl] = _rms(zm[:, sl], mqg).astype(qm_ref.dtype)

    zs = _dot(xn, ws_ref[...])
    wt = zs.T
    wt_ref[...] = wt
    cos4 = cos_ref[...]
    sin4 = sin_ref[...]
    lane = _lane_iota(zs.shape)
    first_half = (lane & (IDX_DIM // 2)) == 0
    yk = _rope64(zs, cos4, sin4, first_half)
    kd_ref[...] = jnp.where(lane < IDX_DIM, yk, pltpu.roll(yk, IDX_DIM, 1)).astype(kd_ref.dtype)

    zq = _dot_nt(wq_ref[...], xn)
    cost = cost_ref[...]
    sint = sint_ref[...]
    scale = (IDX_DIM ** -0.5) * (IDX_HEADS ** -0.5)
    q4 = IDX_DIM // 2
    for c in range(zq.shape[0] // LANES):
        n = zq[c * LANES:(c + 1) * LANES, :]
        rot = jnp.concatenate([n[q4:2 * q4], n[0:q4], n[3 * q4:4 * q4], n[2 * q4:3 * q4]], axis=0)
        y = n * cost + rot * sint
        w2 = jnp.concatenate([jnp.broadcast_to(wt[IDX_DIM + 2 * c + j:IDX_DIM + 2 * c + j + 1, :], (IDX_DIM, n.shape[1]))
                              for j in range(2)], axis=0)
        qi_ref[c * LANES:(c + 1) * LANES, :] = (y * (w2 * scale)).astype(qi_ref.dtype)


def _memkv_kernel(x_ref, g_ref, w_ref, hg_ref, o_ref):
    j = pl.program_id(0)
    xn = _rms(x_ref[...], g_ref[...]).astype(MXU_DTYPE)
    z = _dot(xn, w_ref[...])

    @pl.when(j == 0)
    def _():
        hg = hg_ref[...]
        for h in range(M_HEADS):
            sl = slice(h * HEAD_DIM, (h + 1) * HEAD_DIM)
            o_ref[:, sl] = _rms(z[:, sl], hg).astype(o_ref.dtype)

    @pl.when(j == 1)
    def _():
        o_ref[...] = z.astype(o_ref.dtype)


def _select_kernel(qi_ref, wt_ref, kd_ref, mask_ref, lhs_ref, lo_ref, hi_ref, key_ref, *, tq, tk, seq, top_k):
    qt = pl.program_id(1)
    t0 = qt * tq
    n_kt = (t0 + tq + tk - 1) // tk
    hgrp = 4
    rb = 128
    n_rb = tk // rb
    cr = 32

    feat = lax.broadcasted_iota(I32, (LANES, tq), 0)
    for h in range(IDX_HEADS):
        pair = qi_ref[(h // 2) * LANES:(h // 2 + 1) * LANES, :]
        own = (feat < IDX_DIM) if h % 2 == 0 else (feat >= IDX_DIM)
        lhs_ref[:, h * tq:(h + 1) * tq] = jnp.where(own, pair, jnp.zeros_like(pair))
    w = wt_ref[IDX_DIM:IDX_DIM + IDX_HEADS, :]
    lo_ref[...] = jnp.where(w > 0, 0.0, -jnp.inf).astype(F32)
    hi_ref[...] = jnp.where(w > 0, jnp.inf, 0.0).astype(F32)

    q_lim = ((t0 + _lane_iota((1, tq))) // CHUNK + 1) * CHUNK

    def rows(kt, r):
        return pl.ds(pl.multiple_of(kt * tk + r * rb, rb), rb)

    def score_tile(kt, carry):
        k0 = pl.multiple_of(kt * tk, tk)
        kd = kd_ref[pl.ds(k0, tk), :]
        accs = [jnp.zeros((rb, tq), F32) for _ in range(n_rb)]
        for g in range(IDX_HEADS // hgrp):
            raw = _dot(kd, lhs_ref[:, g * hgrp * tq:(g + 1) * hgrp * tq])
            for hh in range(hgrp):
                h = g * hgrp + hh
                lo = lo_ref[h:h + 1, :]
                hi = hi_ref[h:h + 1, :]
                for r in range(n_rb):
                    blk = raw[r * rb:(r + 1) * rb, hh * tq:(hh + 1) * tq]
                    accs[r] = accs[r] + jnp.minimum(jnp.maximum(blk, lo), hi)
        for r in range(n_rb):
            bits = pltpu.bitcast(accs[r], I32)
            key = bits ^ ((bits >> 31) & 0x7FFFFFFF)
            key_ref[rows(kt, r), :] = jnp.where(k0 + r * rb + lax.broadcasted_iota(I32, (rb, tq), 0) < q_lim, key, INT_MIN)
        return carry

    lax.fori_loop(0, n_kt, score_tile, 0)

    def srow(kt, r):
        return kt * tk + r * rb + lax.broadcasted_iota(I32, (rb, tq), 0)

    def count_keys(ind_fn):
        def tile(kt, cnt):
            for r in range(n_rb):
                ind = ind_fn(key_ref[rows(kt, r), :], kt, r)
                for c in range(rb // cr):
                    cnt = cnt + ind[c * cr:(c + 1) * cr]
            return cnt

        def pair(p, cnt):
            return tile(2 * p + 1, tile(2 * p, cnt))

        cnt = lax.fori_loop(0, n_kt // 2, pair, jnp.zeros((cr, tq), I32))
        cnt = lax.fori_loop(n_kt // 2 * 2, n_kt, tile, cnt)
        return jnp.sum(cnt, axis=0, keepdims=True)

    def bit_step(i, carry):
        prefix, n_ge = carry
        cand_u = prefix | lax.shift_left(jnp.int32(1), jnp.int32(31) - i)
        cand = cand_u ^ INT_MIN
        total = count_keys(lambda key, kt, r: jnp.where(key >= cand, 1, 0))
        keep = total >= top_k
        return jnp.where(keep, cand_u, prefix), jnp.where(keep, total, n_ge)

    group = 4
    few = q_lim <= top_k

    def bit_group(state):
        g, prefix, n_ge, _ = state
        for j in range(group):
            prefix, n_ge = bit_step(g * group + j, (prefix, n_ge))
        open_ = jnp.logical_and(n_ge != top_k, jnp.logical_not(few))
        return g + 1, prefix, n_ge, jnp.max(open_.astype(I32))

    zero_row = jnp.zeros((1, tq), I32)
    state = (jnp.int32(0), zero_row, zero_row, jnp.int32(1))
    _, prefix, n_ge, _ = lax.while_loop(lambda st: jnp.logical_and(st[0] < 32 // group, st[3] > 0), bit_group, state)
    tau = prefix ^ INT_MIN

    def write_mask(sel_fn):
        def write_tile(kt, carry):
            for r in range(n_rb):
                key = key_ref[rows(kt, r), :]
                sel = jnp.logical_and(sel_fn(key, kt, r), key > INT_MIN)
                mask_ref[rows(kt, r), :] = sel.astype(mask_ref.dtype)
            return carry

        lax.fori_loop(0, n_kt, write_tile, 0)

    tied = n_ge > top_k
    any_tied = jnp.max(tied.astype(I32))

    @pl.when(any_tied == 0)
    def _():
        write_mask(lambda key, kt, r: key >= tau)

    @pl.when(any_tied > 0)
    def _():
        n_gt = count_keys(lambda key, kt, r: jnp.where(key > tau, 1, 0))
        need = jnp.where(tied, top_k - n_gt, 2 ** 31 - 1)
        idx_bits = max(1, (seq - 1).bit_length())

        def idx_step(i, p):
            cand = p | lax.shift_left(jnp.int32(1), jnp.int32(idx_bits - 1) - i)
            before = count_keys(lambda key, kt, r: jnp.where(jnp.logical_and(key == tau, srow(kt, r) < cand), 1, 0))
            return jnp.where(before < need, cand, p)

        cut = lax.fori_loop(0, idx_bits, idx_step, zero_row)
        write_mask(lambda key, kt, r: jnp.logical_or(key > tau, jnp.logical_and(key == tau, srow(kt, r) <= cut)))

    def zero_tile(kt, carry):
        mask_ref[pl.ds(pl.multiple_of(kt * tk, tk), tk), :] = jnp.zeros((tk, tq), mask_ref.dtype)
        return carry

    lax.fori_loop(n_kt, seq // tk, zero_tile, 0)


def _attn_kernel(qt_tab, kt_tab, q_ref, k_ref, vt_ref, mask_ref, o_ref, m_ref, l_ref, acc_ref, s_ref, *, tq, tk):
    step = pl.program_id(1)
    qt = qt_tab[step]
    kt = kt_tab[step]
    last_kt = ((qt + 1) * tq - 1) // tk

    @pl.when(kt == 0)
    def _():
        m_ref[...] = jnp.full(m_ref.shape, NEG_BIG, F32)
        l_ref[...] = jnp.zeros(l_ref.shape, F32)
        acc_ref[...] = jnp.zeros(acc_ref.shape, F32)

    bias = jnp.where(mask_ref[...].astype(I32) != 0, 0.0, -jnp.inf).astype(F32)
    hs = lambda h: slice(h * HEAD_DIM, (h + 1) * HEAD_DIM)

    def logits(h):
        s_ref[h % 2] = _dot(k_ref[:, hs(h)], q_ref[hs(h), :])

    logits(0)
    for h in range(A_HEADS):
        if h + 1 < A_HEADS:
            logits(h + 1)
        s = s_ref[h % 2] + bias
        m_prev = m_ref[h:h + 1, :]
        m_new = jnp.maximum(m_prev, jnp.max(s, axis=0, keepdims=True))
        alpha = jnp.exp2(m_prev - m_new)
        p = jnp.exp2(s - m_new)
        l_ref[h:h + 1, :] = alpha * l_ref[h:h + 1, :] + jnp.sum(p, axis=0, keepdims=True)
        acc_ref[hs(h), :] = alpha * acc_ref[hs(h), :] + _dot(vt_ref[hs(h), :], p.astype(MXU_DTYPE))
        m_ref[h:h + 1, :] = m_new

    @pl.when(kt == last_kt)
    def _():
        for h in range(A_HEADS):
            o_ref[:, hs(h)] = (acc_ref[hs(h), :] / l_ref[h:h + 1, :]).T.astype(o_ref.dtype)


def _mix_kernel(x_ref, a_ref, pu_ref, halo_ref, qm_ref, kvm_ref, pw_ref, ps_ref, wo_ref, fg_ref,
                h_ref, hn_ref, mix_ref, *, tm, seq):
    i = pl.program_id(0)
    tiles_per_seq = seq // tm
    pos0 = (i % tiles_per_seq) * tm
    halo_rows = halo_ref.shape[0]
    a_w = a_ref.shape[1]
    p_w = pu_ref.shape[1]
    grp = p_w // len(POOL_WINDOWS)

    halo = jnp.where(pos0 > 0, halo_ref[...], 0.0)
    ext = jnp.concatenate([halo, pu_ref[...]], axis=0)
    pos = pos0 + lax.broadcasted_iota(I32, (tm, grp), 0)
    run = ext
    width = 1
    for g, w in enumerate(POOL_WINDOWS):
        while width < w:
            run = run + pltpu.roll(run, width, 0)
            width *= 2
        sl = slice(g * grp, (g + 1) * grp)
        wsum = run[halo_rows:, sl]
        cnt = jnp.minimum(pos + 1, w).astype(F32)
        d = wsum / cnt - pu_ref[:, sl]
        pg = _dot(d.astype(MXU_DTYPE), pw_ref[g]) * ps_ref[:, sl]
        mix_ref[:, sl] = pg.astype(mix_ref.dtype)

    m_w = qm_ref.shape[1]
    for h in range(M_HEADS):
        sl = slice(h * HEAD_DIM, (h + 1) * HEAD_DIM)
        s = _dot_nt(qm_ref[:, sl], kvm_ref[:, sl])
        p = jnp.exp(s - jnp.max(s, axis=1, keepdims=True))
        o = _dot(p.astype(MXU_DTYPE), kvm_ref[:, m_w + h * HEAD_DIM:m_w + (h + 1) * HEAD_DIM])
        o = o / jnp.sum(p, axis=1, keepdims=True)
        mix_ref[:, p_w + h * HEAD_DIM:p_w + (h + 1) * HEAD_DIM] = o.astype(mix_ref.dtype)

    out = x_ref[...] + _dot(a_ref[...], wo_ref[0:a_w, :]) + _dot(mix_ref[...], wo_ref[a_w:, :])
    h_ref[...] = out
    hn_ref[...] = _rms(out, fg_ref[...]).astype(hn_ref.dtype)


def _top16_rows(s, row):
    rank = jnp.full(s.shape, float(PEER_TOPK), F32)
    big = float(s.shape[0])
    vals = []
    for k in range(PEER_TOPK):
        m = jnp.max(s, axis=0, keepdims=True)
        first = jnp.min(jnp.where(s == m, row, big), axis=0, keepdims=True)
        hit = row == first
        rank = jnp.where(hit, float(k), rank)
        s = jnp.where(hit, -jnp.inf, s)
        vals.append(m)
    return jnp.concatenate(vals, axis=0), rank


def _route_kernel(hn_ref, wq_ref, k1_ref, k2_ref, r2_ref, e2_ref, n_ref, c_ref, *, tm):
    q = _dot(hn_ref[...], wq_ref[...]).astype(MXU_DTYPE)
    row = lax.broadcasted_iota(I32, (N_KEYS, tm), 0).astype(F32)
    sub = lax.broadcasted_iota(I32, (8, tm), 0).astype(F32)
    half = N_KEYS
    for h in range(PEER_HEADS):
        s1 = _dot_nt(k1_ref[...], q[:, h * 2 * half:h * 2 * half + half])
        s2 = _dot_nt(k2_ref[...], q[:, h * 2 * half + half:(h + 1) * 2 * half])
        v1, rank1 = _top16_rows(s1, row)
        v2, rank2 = _top16_rows(s2, row)

        blocks, flats = [], []
        for half_a in range(2):
            blocks.append(v1[half_a * 8:(half_a + 1) * 8] + v2[0:1])
            flats.append((sub + half_a * 8) * PEER_TOPK)
        for b in range(1, 8):
            blk = v1[0:8] + v2[b:b + 1]
            blocks.append(jnp.where(sub < PEER_TOPK // (b + 1), blk, -jnp.inf))
            flats.append(sub * PEER_TOPK + b)
        blocks.append(v1[0:1] + v2[8:16])
        flats.append(sub + 8)
        cand = jnp.concatenate(blocks, axis=0)
        flat = jnp.concatenate(flats, axis=0)
        cand0 = cand
        sel = jnp.zeros(cand.shape, jnp.bool_)
        for _ in range(PEER_TOPK):
            m = jnp.max(cand, axis=0, keepdims=True)
            first = jnp.min(jnp.where(cand == m, flat, float(PEER_TOPK * PEER_TOPK)), axis=0, keepdims=True)
            hit = flat == first
            sel = jnp.logical_or(sel, hit)
            cand = jnp.where(hit, -jnp.inf, cand)
        top = v1[0:1] + v2[0:1]
        z = jnp.sum(jnp.where(sel, jnp.exp(cand0 - top), 0.0), axis=0, keepdims=True)
        self32 = sel.astype(F32)
        nb_lo = self32[0:8]
        for blk in range(2, 9):
            nb_lo = nb_lo + self32[blk * 8:(blk + 1) * 8]
        tail = jnp.sum(self32[72:80], axis=0, keepdims=True)
        nb_lo = nb_lo + jnp.where(sub == 0, tail, 0.0)
        nb = jnp.concatenate([nb_lo, self32[8:16]], axis=0)

        n_dense = jnp.zeros((N_KEYS, tm), F32)
        for a in range(PEER_TOPK):
            n_dense = jnp.where(rank1 == float(a), nb[a:a + 1], n_dense)
        r2_ref[h] = rank2
        e2_ref[h] = jnp.exp(s2 - v2[0:1])
        n_ref[h] = n_dense
        c_ref[h] = jnp.exp(s1 - v1[0:1]) / z


def _gelu(x):
    return 0.5 * x * (1.0 + lax.erf(x * (2.0 ** -0.5)))


def _peer_kernel(hn_ref, u_ref, vt_ref, r2_ref, e2_ref, n_ref, c_ref, h_ref, o_ref, acc_ref, at_ref, *, tb, eb):
    ei = pl.program_id(1)

    @pl.when(ei == 0)
    def _():
        acc_ref[...] = jnp.zeros(acc_ref.shape, F32)

    ht = _dot_nt(u_ref[...], hn_ref[...])
    ipb = eb // N_KEYS
    sub = 8
    base = pl.multiple_of((ei * ipb) // sub * sub, sub)
    off = (ei * ipb) % sub
    for il in range(ipb):
        for tcol in range(tb // LANES):
            cs = slice(tcol * LANES, (tcol + 1) * LANES)
            gate = jnp.zeros((N_KEYS, LANES), F32)
            for h in range(PEER_HEADS):
                n8 = n_ref[h, pl.ds(base, sub), cs]
                c8 = c_ref[h, pl.ds(base, sub), cs]
                nrow = n8[il:il + 1]
                crow = c8[il:il + 1]
                for o in range(ipb, sub, ipb):
                    nrow = jnp.where(off == o, n8[o + il:o + il + 1], nrow)
                    crow = jnp.where(off == o, c8[o + il:o + il + 1], crow)
                gate = gate + jnp.where(r2_ref[h, :, cs] < nrow, e2_ref[h, :, cs], 0.0) * crow
            a = gate * _gelu(ht[il * N_KEYS:(il + 1) * N_KEYS, cs])
            at_ref[il * N_KEYS:(il + 1) * N_KEYS, cs] = a.astype(at_ref.dtype)
    acc_ref[...] += _dot(vt_ref[...], at_ref[...])

    @pl.when(ei == pl.num_programs(1) - 1)
    def _():
        o_ref[...] = h_ref[...] + acc_ref[...].T


def _rope_tables(seq):
    pos = jnp.arange(seq, dtype=F32)[:, None]

    def table(half):
        inv = ROPE_THETA ** (-jnp.arange(half, dtype=F32) / half)
        ang = pos * inv[None, :]
        return jnp.cos(ang), jnp.sin(ang)

    c64, s64 = table(HEAD_DIM // 2)
    c32, s32 = table(IDX_DIM // 2)
    cos_a = jnp.concatenate([c64, c64], axis=1)
    sin_a = jnp.concatenate([-s64, s64], axis=1)
    cos_i = jnp.concatenate([c32, c32, c32, c32], axis=1)
    sin_i = jnp.concatenate([-s32, s32, -s32, s32], axis=1)
    return cos_a, sin_a, cos_i, sin_i


def _attn_steps(seq, tq, tk):
    qts, kts = [], []
    for qt in range(seq // tq):
        for kt in range(((qt + 1) * tq - 1) // tk + 1):
            qts.append(qt)
            kts.append(kt)
    return np.asarray(qts, np.int32), np.asarray(kts, np.int32)


def _full(shape):
    return pl.BlockSpec(shape, lambda *_: (0,) * len(shape))


def _layer(x, mem, attn_g, w_in, q_g, k_g, pool_w, pool_s, mem_g, w_mem_kv, mq_g, mk_g, w_out,
           ffn_g, peer_wq, sk1, sk2, peer_u, peer_v):
    b, seq, d = x.shape
    t = b * seq
    mem_len = mem.shape[1]
    a_w = A_HEADS * HEAD_DIM
    i_w = IDX_HEADS * IDX_DIM
    p_w = pool_w.shape[0] * pool_w.shape[1]
    m_w = M_HEADS * HEAD_DIM
    top_k = min(TOP_K_MAX, seq // 4)
    md = MXU_DTYPE

    x2 = x.reshape(t, d)
    row = lambda v: v.reshape(1, -1).astype(F32)
    cos_a, sin_a, cos_i, sin_i = _rope_tables(seq)

    o = 0
    w_qt = w_in[:, o:o + a_w].T.astype(md); o += a_w
    w_k = w_in[:, o:o + a_w].astype(md); o += a_w
    w_vt = w_in[:, o:o + a_w].T.astype(md); o += a_w
    w_qi = w_in[:, o:o + i_w].T.astype(md); o += i_w
    w_small = jnp.pad(w_in[:, o:o + IDX_DIM + IDX_HEADS], ((0, 0), (0, LANES - IDX_DIM - IDX_HEADS))).astype(md)
    o += IDX_DIM + IDX_HEADS
    w_pu = w_in[:, o:o + p_w].astype(md); o += p_w
    w_qm = w_in[:, o:o + m_w].astype(md)

    tm = min(512, seq)
    n_rt = t // tm
    pos_blk = lambda i: i % (seq // tm)
    gq = jnp.broadcast_to((q_g.astype(F32) * (HEAD_DIM ** -0.5 * np.log2(np.e)))[:, None], (HEAD_DIM, tm))
    rope_rows = pl.BlockSpec((tm, HEAD_DIM), lambda i: (pos_blk(i), 0))
    rope_cols = pl.BlockSpec((HEAD_DIM, tm), lambda i: (0, pos_blk(i)))
    qt, kk = pl.pallas_call(
        _qk_kernel,
        grid=(n_rt,),
        in_specs=[pl.BlockSpec((tm, d), lambda i: (i, 0)), _full((1, d)), _full((a_w, d)), _full((d, a_w)),
                  _full((HEAD_DIM, tm)), _full((1, HEAD_DIM)), rope_rows, rope_rows, rope_cols, rope_cols],
        out_specs=[pl.BlockSpec((a_w, tm), lambda i: (0, i)), pl.BlockSpec((tm, a_w), lambda i: (i, 0))],
        out_shape=[jax.ShapeDtypeStruct((a_w, t), md), jax.ShapeDtypeStruct((t, a_w), md)],
        compiler_params=_cparams("parallel"),
        name="qk_proj",
    )(x2, row(attn_g), w_qt, w_k, gq, row(k_g), cos_a, sin_a, cos_a.T, sin_a.T)

    tmi = min(256, seq)
    tok = lambda w: pl.BlockSpec((tmi, w), lambda i: (i, 0))
    tok_t = lambda w: pl.BlockSpec((w, tmi), lambda i: (0, i))
    rope_spec = pl.BlockSpec((tmi, LANES), lambda i: (i % (seq // tmi), 0))
    rope_spec_t = pl.BlockSpec((LANES, tmi), lambda i: (0, i % (seq // tmi)))
    vt, qi, kd, wt, pu, qm = pl.pallas_call(
        _aux_proj_kernel,
        grid=(t // tmi,),
        in_specs=[tok(d), _full((1, d)), _full((a_w, d)), _full((i_w, d)), _full((d, LANES)),
                  rope_spec, rope_spec, rope_spec_t, rope_spec_t, _full((d, p_w)), _full((d, m_w)), _full((1, HEAD_DIM))],
        out_specs=[tok_t(a_w), tok_t(i_w), tok(LANES), tok_t(LANES), tok(p_w), tok(m_w)],
        out_shape=[jax.ShapeDtypeStruct((a_w, t), md),
                   jax.ShapeDtypeStruct((i_w, t), md),
                   jax.ShapeDtypeStruct((t, LANES), md),
                   jax.ShapeDtypeStruct((LANES, t), F32),
                   jax.ShapeDtypeStruct((t, p_w), F32),
                   jax.ShapeDtypeStruct((t, m_w), md)],
        compiler_params=_cparams("parallel"),
        name="aux_proj",
    )(x2, row(attn_g), w_vt, w_qi, w_small, cos_i, sin_i, cos_i.T, sin_i.T, w_pu, w_qm,
      row(mq_g) * (HEAD_DIM ** -0.5))

    tmem = b * mem_len
    kvm = pl.pallas_call(
        _memkv_kernel,
        grid=(2,),
        in_specs=[_full((tmem, d)), _full((1, d)),
                  pl.BlockSpec((d, m_w), lambda j: (0, j)), _full((1, HEAD_DIM))],
        out_specs=pl.BlockSpec((tmem, m_w), lambda j: (0, j)),
        out_shape=jax.ShapeDtypeStruct((tmem, 2 * m_w), md),
        compiler_params=_cparams("arbitrary"),
        name="memkv_proj",
    )(mem.reshape(tmem, d), row(mem_g), w_mem_kv.astype(md), row(mk_g))

    tq, tks = 256, min(1024, seq)
    n_qt = seq // tq
    mask = pl.pallas_call(
        functools.partial(_select_kernel, tq=tq, tk=tks, seq=seq, top_k=top_k),
        grid=(b, n_qt),
        in_specs=[pl.BlockSpec((i_w, tq), lambda bi, qt: (0, bi * n_qt + qt)),
                  pl.BlockSpec((LANES, tq), lambda bi, qt: (0, bi * n_qt + qt)),
                  pl.BlockSpec((None, seq, LANES), lambda bi, qt: (bi, 0, 0))],
        out_specs=pl.BlockSpec((None, seq, tq), lambda bi, qt: (bi, 0, qt)),
        out_shape=jax.ShapeDtypeStruct((b, seq, seq), jnp.int8),
        scratch_shapes=[pltpu.VMEM((LANES, IDX_HEADS * tq), md),
                        pltpu.VMEM((IDX_HEADS, tq), F32),
                        pltpu.VMEM((IDX_HEADS, tq), F32),
                        pltpu.VMEM((seq, tq), I32)],
        compiler_params=_cparams("parallel", "arbitrary"),
        name="dsa_select",
    )(qi, wt, kd.reshape(b, seq, LANES))

    tqa, tka = min(512, seq), min(1024, seq)
    n_kta = seq // tka
    qts, kts = _attn_steps(seq, tqa, tka)
    n_qta = seq // tqa
    a_out = pl.pallas_call(
        functools.partial(_attn_kernel, tq=tqa, tk=tka),
        grid_spec=pltpu.PrefetchScalarGridSpec(
            num_scalar_prefetch=2,
            grid=(b, len(qts)),
            in_specs=[pl.BlockSpec((a_w, tqa), lambda bi, s, qt, kt: (0, bi * n_qta + qt[s])),
                      pl.BlockSpec((None, tka, a_w), lambda bi, s, qt, kt: (bi, kt[s], 0)),
                      pl.BlockSpec((a_w, tka), lambda bi, s, qt, kt: (0, bi * n_kta + kt[s])),
                      pl.BlockSpec((None, tka, tqa), lambda bi, s, qt, kt: (bi, kt[s], qt[s]))],
            out_specs=pl.BlockSpec((None, tqa, a_w), lambda bi, s, qt, kt: (bi, qt[s], 0)),
            scratch_shapes=[pltpu.VMEM((A_HEADS, tqa), F32),
                            pltpu.VMEM((A_HEADS, tqa), F32),
                            pltpu.VMEM((a_w, tqa), F32),
                            pltpu.VMEM((2, tka, tqa), F32)]),
        out_shape=jax.ShapeDtypeStruct((b, seq, a_w), md),
        compiler_params=_cparams("parallel", "arbitrary"),
        name="dsa_attn",
    )(jnp.asarray(qts), jnp.asarray(kts), qt, kk.reshape(b, seq, a_w), vt, mask)

    tmx = min(512, seq)
    halo_rows = POOL_WINDOWS[-1]
    hsub = tmx // halo_rows
    h_res, hn = pl.pallas_call(
        functools.partial(_mix_kernel, tm=tmx, seq=seq),
        grid=(t // tmx,),
        in_specs=[pl.BlockSpec((tmx, d), lambda i: (i, 0)),
                  pl.BlockSpec((tmx, a_w), lambda i: (i, 0)),
                  pl.BlockSpec((tmx, p_w), lambda i: (i, 0)),
                  pl.BlockSpec((halo_rows, p_w), lambda i: (jnp.maximum(i * hsub - 1, 0), 0)),
                  pl.BlockSpec((tmx, m_w), lambda i: (i, 0)),
                  pl.BlockSpec((mem_len, 2 * m_w), lambda i: (i // (seq // tmx), 0)),
                  _full(pool_w.shape), _full((1, p_w)), _full((d, d)), _full((1, d))],
        out_specs=[pl.BlockSpec((tmx, d), lambda i: (i, 0)),
                   pl.BlockSpec((tmx, d), lambda i: (i, 0))],
        out_shape=[jax.ShapeDtypeStruct((t, d), F32), jax.ShapeDtypeStruct((t, d), md)],
        scratch_shapes=[pltpu.VMEM((tmx, p_w + m_w), md)],
        compiler_params=_cparams("parallel"),
        name="mix_out",
    )(x2, a_out.reshape(t, a_w), pu, pu, qm, kvm, pool_w.astype(md), row(pool_s), w_out.astype(md), row(ffn_g))

    tmr = 256
    rshape = jax.ShapeDtypeStruct((PEER_HEADS, N_KEYS, t), F32)
    rspec = pl.BlockSpec((PEER_HEADS, N_KEYS, tmr), lambda i: (0, 0, i))
    r2, e2, nsel, csel = pl.pallas_call(
        functools.partial(_route_kernel, tm=tmr),
        grid=(t // tmr,),
        in_specs=[pl.BlockSpec((tmr, d), lambda i: (i, 0)),
                  _full(peer_wq.shape), _full(sk1.shape), _full(sk2.shape)],
        out_specs=[rspec, rspec, rspec, rspec],
        out_shape=[rshape, rshape, rshape, rshape],
        compiler_params=_cparams("parallel"),
        name="peer_route",
    )(hn, peer_wq.astype(md), sk1.astype(md), sk2.astype(md))

    tb, eb = 512, 512
    n_exp = peer_u.shape[0]
    gspec = pl.BlockSpec((PEER_HEADS, N_KEYS, tb), lambda ti, ei: (0, 0, ti))
    out = pl.pallas_call(
        functools.partial(_peer_kernel, tb=tb, eb=eb),
        grid=(t // tb, n_exp // eb),
        in_specs=[pl.BlockSpec((tb, d), lambda ti, ei: (ti, 0)),
                  pl.BlockSpec((eb, d), lambda ti, ei: (ei, 0)),
                  pl.BlockSpec((d, eb), lambda ti, ei: (0, ei)),
                  gspec, gspec, gspec, gspec,
                  pl.BlockSpec((tb, d), lambda ti, ei: (ti, 0))],
        out_specs=pl.BlockSpec((tb, d), lambda ti, ei: (ti, 0)),
        out_shape=jax.ShapeDtypeStruct((t, d), F32),
        scratch_shapes=[pltpu.VMEM((d, tb), F32), pltpu.VMEM((eb, tb), md)],
        compiler_params=_cparams("parallel", "arbitrary"),
        name="peer_experts",
    )(hn, peer_u.astype(md), peer_v.T.astype(md), r2, e2, nsel, csel, h_res)
    return out.reshape(b, seq, d)


def kernel(x, mem, attn_norm_gain, w_in, q_norm_gain, k_norm_gain, pool_w, pool_scale, mem_norm_gain,
           w_mem_kv, mq_norm_gain, mk_norm_gain, w_out, ffn_norm_gain, peer_w_q, peer_sub_keys_1,
           peer_sub_keys_2, peer_u, peer_v):
    for l in range(attn_norm_gain.shape[0]):
        x = _layer(x, mem, attn_norm_gain[l], w_in[l], q_norm_gain[l], k_norm_gain[l], pool_w[l],
                   pool_scale[l], mem_norm_gain[l], w_mem_kv[l], mq_norm_gain[l], mk_norm_gain[l],
                   w_out[l], ffn_norm_gain[l], peer_w_q[l], peer_sub_keys_1[l], peer_sub_keys_2[l],
                   peer_u[l], peer_v[l])
    return x
```
